```python
import math, functools
import jax, jax.numpy as jnp
from jax import lax
import numpy as np

D_MODEL = 1024
BATCH = 1
SEQ = 16384
DEPTH = 2
DEC_BATCH = 16
DEC_SEQ = 64
PAST_LEN = 2048

CHUNK = 64
HEAD_DIM = 64
N_HEADS_A = 16
N_KV_A = 2
WINDOW_A = 128
PAST_CHUNKS_A = WINDOW_A // CHUNK
N_HEADS_B = 16
PAST_CHUNKS_B = 8
REL_CLIP_B = 128
T5_BUCKETS = 32
T5_MAX_DIST = 128
WIDTH_A = N_HEADS_A * HEAD_DIM
WIDTH_B = N_HEADS_B * HEAD_DIM
IN_A = 2 * WIDTH_A + 2 * N_KV_A * HEAD_DIM
IN_B = 2 * WIDTH_B + 2 * N_HEADS_B * HEAD_DIM
EPS = 1e-6
NEG_INF = -1e30

kernel_name = 'hybrid_streaming_swa_sink_chunkband_step'


def rms_norm(x, g):
    xf = x.astype(jnp.float32)
    y = xf * lax.rsqrt(jnp.mean(xf * xf, axis=-1, keepdims=True) + EPS)
    return (y * g.astype(jnp.float32)).astype(x.dtype)


def t5_bucket(rel):
    nb = T5_BUCKETS // 2
    max_exact = nb // 2
    ret = jnp.where(rel > 0, nb, 0)
    n = jnp.abs(rel)
    nf = jnp.maximum(n, 1).astype(jnp.float32)
    large = max_exact + (jnp.log(nf / max_exact) / math.log(T5_MAX_DIST / max_exact)
                         * (nb - max_exact)).astype(jnp.int32)
    large = jnp.minimum(large, nb - 1)
    return ret + jnp.where(n < max_exact, n, large)


def t5_bias(table, q_pos, k_pos):
    b = table[t5_bucket(k_pos[None, :] - q_pos[:, None])]
    return jnp.transpose(b, (2, 0, 1)).astype(jnp.float32)


def clipped_rel_bias(table, q_pos, k_pos):
    idx = jnp.clip(q_pos[:, None] - k_pos[None, :], -REL_CLIP_B, REL_CLIP_B) + REL_CLIP_B
    return table[:, idx].astype(jnp.float32)


def attend(q, k, v, bias, valid, sinks):
    b, tq, h, dh = q.shape
    kv = k.shape[2]
    g = h // kv
    tk = k.shape[1]
    qg = q.reshape(b, tq, kv, g, dh)
    s = jnp.einsum('bqkgd,bskd->bkgqs', qg, k).astype(jnp.float32) * (dh ** -0.5)
    s = s + bias.reshape(kv, g, tq, tk)
    if valid is not None:
        s = jnp.where(valid, s, NEG_INF)
    if sinks is not None:
        sink = sinks.astype(jnp.float32).reshape(kv, g, 1, 1)
        m = jnp.maximum(jnp.max(s, axis=-1, keepdims=True), sink)
        p = jnp.exp(s - m)
        p = p / (jnp.sum(p, axis=-1, keepdims=True) + jnp.exp(sink - m))
    else:
        p = jax.nn.softmax(s, axis=-1)
    o = jnp.einsum('bkgqs,bskd->bqkgd', p.astype(v.dtype), v)
    return o.reshape(b, tq, h, dh)


def project(x, norm_g, w_in, q_g, k_g, n_heads, n_kv):
    h = rms_norm(x, norm_g)
    p = h @ w_in
    wq = n_heads * HEAD_DIM
    wkv = n_kv * HEAD_DIM
    q, k, v, gate = jnp.split(p, [wq, wq + wkv, wq + 2 * wkv], axis=-1)
    lead = x.shape[:-1]
    q = rms_norm(q.reshape(*lead, n_heads, HEAD_DIM), q_g)
    k = rms_norm(k.reshape(*lead, n_kv, HEAD_DIM), k_g)
    v = v.reshape(*lead, n_kv, HEAD_DIM)
    return q, k, v, gate


def prompt_band_attention(q, k, v, past_chunks, bias_fn, sinks):
    b, s, h, dh = q.shape
    nc = s // CHUNK
    pad = past_chunks * CHUNK
    band = pad + CHUNK
    kp = jnp.pad(k, ((0, 0), (pad, 0), (0, 0), (0, 0)))
    vp = jnp.pad(v, ((0, 0), (pad, 0), (0, 0), (0, 0)))
    q_pos = pad + jnp.arange(CHUNK, dtype=jnp.int32)
    k_pos = jnp.arange(band, dtype=jnp.int32)
    bias = bias_fn(q_pos, k_pos)

    def one_chunk(c):
        start = c * CHUNK
        qc = lax.dynamic_slice_in_dim(q, start, CHUNK, axis=1)
        kc = lax.dynamic_slice_in_dim(kp, start, band, axis=1)
        vc = lax.dynamic_slice_in_dim(vp, start, band, axis=1)
        valid = (start + k_pos) >= pad
        return attend(qc, kc, vc, bias, valid, sinks)

    o = lax.map(one_chunk, jnp.arange(nc, dtype=jnp.int32))
    return jnp.transpose(o, (1, 0, 2, 3, 4)).reshape(b, s, h * dh)


def sample_band_attention(q, k_new, v_new, k_cache, v_cache, bias_fn, sinks):
    b, tn, h, dh = q.shape
    L = k_cache.shape[1]
    kb = jnp.concatenate([k_cache, k_new], axis=1)
    vb = jnp.concatenate([v_cache, v_new], axis=1)
    q_pos = L + jnp.arange(tn, dtype=jnp.int32)
    k_pos = jnp.arange(L + tn, dtype=jnp.int32)
    o = attend(q, kb, vb, bias_fn(q_pos, k_pos), None, sinks)
    return o.reshape(b, tn, h * dh), kb[:, tn:], vb[:, tn:]


def gated_attention_layer(x_p, x_s, ck, cv, norm_g, w_in, q_g, k_g, w_out,
                          n_heads, n_kv, past_chunks, bias_fn, sinks):
    qp, kp, vp, gp = project(x_p, norm_g, w_in, q_g, k_g, n_heads, n_kv)
    op = prompt_band_attention(qp, kp, vp, past_chunks, bias_fn, sinks)
    y_p = x_p + (op * jax.nn.silu(gp)) @ w_out
    qs, ks, vs, gs = project(x_s, norm_g, w_in, q_g, k_g, n_heads, n_kv)
    os_, nk, nv = sample_band_attention(qs, ks, vs, ck, cv, bias_fn, sinks)
    y_s = x_s + (os_ * jax.nn.silu(gs)) @ w_out
    keep = past_chunks * CHUNK
    return y_p, y_s, kp[:, -keep:], vp[:, -keep:], nk, nv


def setup_inputs(seed: int = 0) -> dict:
    key = jax.random.key(seed)
    ks = jax.random.split(key, 20)
    f32 = jnp.float32
    la = min(WINDOW_A, PAST_LEN)
    lb = min(PAST_CHUNKS_B * CHUNK, PAST_LEN)
    nrm = lambda k, shape, s=1.0: (jax.random.normal(k, shape, f32) * s).astype(f32)
    return {
        'x_prompt': nrm(ks[0], (BATCH, SEQ, D_MODEL)),
        'x_sample': nrm(ks[1], (DEC_BATCH, DEC_SEQ, D_MODEL)),
        'cache_k_a': nrm(ks[2], (DEC_BATCH, la, N_KV_A, HEAD_DIM)),
        'cache_v_a': nrm(ks[3], (DEC_BATCH, la, N_KV_A, HEAD_DIM)),
        'cache_k_b': nrm(ks[4], (DEC_BATCH, lb, N_HEADS_B, HEAD_DIM)),
        'cache_v_b': nrm(ks[5], (DEC_BATCH, lb, N_HEADS_B, HEAD_DIM)),
        't5_table': nrm(ks[6], (T5_BUCKETS, N_HEADS_A), 0.3),
        'norm_a': 1.0 + nrm(ks[7], (D_MODEL,), 0.02),
        'w_in_a': nrm(ks[8], (D_MODEL, IN_A), D_MODEL ** -0.5),
        'q_norm_a': 1.0 + nrm(ks[9], (HEAD_DIM,), 0.02),
        'k_norm_a': 1.0 + nrm(ks[10], (HEAD_DIM,), 0.02),
        'sinks_a': nrm(ks[11], (N_HEADS_A,), 0.5),
        'w_out_a': nrm(ks[12], (WIDTH_A, D_MODEL), WIDTH_A ** -0.5),
        'norm_b': 1.0 + nrm(ks[13], (D_MODEL,), 0.02),
        'w_in_b': nrm(ks[14], (D_MODEL, IN_B), D_MODEL ** -0.5),
        'q_norm_b': 1.0 + nrm(ks[15], (HEAD_DIM,), 0.02),
        'k_norm_b': 1.0 + nrm(ks[16], (HEAD_DIM,), 0.02),
        'rel_bias_b': nrm(ks[17], (N_HEADS_B, 2 * REL_CLIP_B + 1), 0.3),
        'w_out_b': nrm(ks[18], (WIDTH_B, D_MODEL), WIDTH_B ** -0.5),
    }


def reference(x_prompt, x_sample, cache_k_a, cache_v_a, cache_k_b, cache_v_b,
              t5_table, norm_a, w_in_a, q_norm_a, k_norm_a, sinks_a, w_out_a,
              norm_b, w_in_b, q_norm_b, k_norm_b, rel_bias_b, w_out_b):
    bias_a = functools.partial(t5_bias, t5_table)
    bias_b = functools.partial(clipped_rel_bias, rel_bias_b)
    x_p, x_s = x_prompt, x_sample
    for i in range(DEPTH):
        if i % 2 == 0:
            x_p, x_s, k_a_p, v_a_p, k_a_s, v_a_s = gated_attention_layer(
                x_p, x_s, cache_k_a, cache_v_a, norm_a, w_in_a, q_norm_a, k_norm_a,
                w_out_a, N_HEADS_A, N_KV_A, PAST_CHUNKS_A, bias_a, sinks_a)
        else:
            x_p, x_s, k_b_p, v_b_p, k_b_s, v_b_s = gated_attention_layer(
                x_p, x_s, cache_k_b, cache_v_b, norm_b, w_in_b, q_norm_b, k_norm_b,
                w_out_b, N_HEADS_B, N_HEADS_B, PAST_CHUNKS_B, bias_b, None)
    return (x_p, x_s, k_a_p, v_a_p, k_b_p, v_b_p, k_a_s, v_a_s, k_b_s, v_b_s)
```

```python
import functools
import math
from typing import NamedTuple

import jax
import jax.numpy as jnp
from jax import lax
from jax.experimental import pallas as pl
from jax.experimental.pallas import tpu as pltpu

D_MODEL = 1024
CHUNK = 64
HEAD_DIM = 64
N_HEADS = 16
REL_CLIP = 128
T5_BUCKETS = 32
T5_MAX_DIST = 128
EPS = 1e-6
NEG_INF = -1e30

LANES = 128
QUAD = 256
PAIR = LANES
MXU_DIM = 256
VMEM_BYTES = 64 * 2**20

BF16 = jnp.bfloat16
F32 = jnp.float32
NT_DIMS = (((1,), (1,)), ((), ()))


class LayerCfg(NamedTuple):
    n_kv: int
    past_chunks: int
    use_sinks: bool

    @property
    def kvd(self):
        return self.n_kv * HEAD_DIM

    @property
    def group(self):
        return N_HEADS // self.n_kv

    @property
    def hist(self):
        return self.past_chunks * CHUNK

    @property
    def pair_win(self):
        return self.hist + PAIR

    @property
    def quad_win(self):
        return self.hist + QUAD

    @property
    def gnorm(self):
        return min(self.kvd, MXU_DIM)


CFG_A = LayerCfg(n_kv=2, past_chunks=2, use_sinks=True)
CFG_B = LayerCfg(n_kv=16, past_chunks=8, use_sinks=False)


def _rms_rows(x, g_row):
    ms = jnp.mean(x * x, axis=-1, keepdims=True)
    return x * lax.rsqrt(ms + EPS) * g_row


def _project_k(h, wk_ref, gmat_ref, gk_ref, cfg):
    k = jnp.dot(h, wk_ref[...], preferred_element_type=F32)
    ksq = (k * k).astype(BF16)
    gm = gmat_ref[...]
    gn = cfg.gnorm
    ss = jnp.concatenate(
        [jnp.dot(ksq[:, c:c + gn], gm, preferred_element_type=F32)
         for c in range(0, cfg.kvd, gn)], axis=1)
    return k * lax.rsqrt(ss * (1.0 / HEAD_DIM) + EPS) * gk_ref[...]


def _project_q(h, wqT_ref, gq_ref, qpad_ref, cfg):
    t = h.shape[0]
    qT = lax.dot_general(wqT_ref[...], h, NT_DIMS, preferred_element_type=F32)
    q3 = qT.reshape(N_HEADS, HEAD_DIM, t)
    ms = jnp.mean(q3 * q3, axis=1, keepdims=True)
    g = gq_ref[...][:, :t] if t < LANES else jnp.concatenate([gq_ref[...]] * (t // LANES), axis=1)
    qn = (q3 * lax.rsqrt(ms + EPS) * g[None]).astype(BF16)
    z = jnp.zeros((HEAD_DIM, t), BF16)
    for hd in range(N_HEADS):
        odd = (hd // cfg.group) % 2
        qpad_ref[hd] = jnp.concatenate([z, qn[hd]] if odd else [qn[hd], z], axis=0)


def _softmax_cols(s, sink):
    m = jnp.max(s, axis=0, keepdims=True)
    if sink is not None:
        m = jnp.maximum(m, sink)
    p = jnp.exp(s - m)
    l = jnp.sum(p, axis=0, keepdims=True)
    if sink is not None:
        l = l + jnp.exp(sink - m)
    return p, 1.0 / l


def _prompt_layer_kernel(*refs, cfg, tile):
    if cfg.use_sinks:
        sinks_ref, refs = refs[0], refs[1:]
    else:
        sinks_ref = None
    (x_ref, ng_ref, wk_ref, wqT_ref, wvT_ref, wgT_ref, woT_ref, gmat_ref, gq_ref, gk_ref,
     bias_ref, y_ref, kkeep_ref, vkeep_ref,
     khist, vhist, qpad, gts, aT, maskbuf) = refs
    t = pl.program_id(0)
    nt = pl.num_programs(0)
    hw, wp, w = cfg.hist, cfg.pair_win, cfg.quad_win
    nq = tile // QUAD

    @pl.when(t == 0)
    def _zero_history():
        khist[:, :hw, :] = jnp.zeros((khist.shape[0], hw, LANES), BF16)
        vhist[:, :, :hw] = jnp.zeros((cfg.n_kv, HEAD_DIM, hw), BF16)

    n_invalid = jnp.maximum(hw - t * tile, 0)
    rows = lax.broadcasted_iota(jnp.int32, (wp, LANES), 0)
    for qi in range(nq):
        for lt in range(2):
            off = qi * QUAD + lt * PAIR
            maskbuf[2 * qi + lt] = jnp.where(rows + off < n_invalid, NEG_INF, 0.0).astype(F32)

    x = x_ref[...]
    h = _rms_rows(x, ng_ref[...]).astype(BF16)

    kn = _project_k(h, wk_ref, gmat_ref, gk_ref, cfg)
    for kb in range(cfg.kvd // LANES):
        khist[kb, hw:hw + tile, :] = kn[:, kb * LANES:(kb + 1) * LANES].astype(BF16)

    _project_q(h, wqT_ref, gq_ref, qpad, cfg)

    vT = lax.dot_general(wvT_ref[...], h, NT_DIMS, preferred_element_type=F32)
    vhist[:, :, hw:hw + tile] = vT.reshape(cfg.n_kv, HEAD_DIM, tile).astype(BF16)

    gT = lax.dot_general(wgT_ref[...], h, NT_DIMS, preferred_element_type=F32)
    gts[...] = (gT * jax.nn.sigmoid(gT)).reshape(N_HEADS, HEAD_DIM, tile)

    if tile >= hw:
        @pl.when(t == nt - 1)
        def _keep():
            kkeep_ref[...] = kn[tile - hw:, :]
            vkeep_ref[...] = vT[:, tile - hw:].T
    else:
        @pl.when(t >= nt - hw // tile)
        def _keep():
            row0 = pl.multiple_of(hw - (nt - t) * tile, tile)
            kkeep_ref[pl.ds(row0, tile), :] = kn
            vkeep_ref[pl.ds(row0, tile), :] = vT.T

    zpad = jnp.zeros((PAIR, LANES), BF16)

    def head(hd, carry):
        kvh = hd // cfg.group
        kb = kvh // 2
        sink = sinks_ref[hd] if cfg.use_sinks else None
        for qi in range(nq):
            r0 = qi * QUAD
            sT = jnp.dot(khist[kb, r0:r0 + w, :], qpad[hd, :, r0:r0 + QUAD],
                         preferred_element_type=F32)
            pcols, linv = [], []
            for lt in range(2):
                s = (sT[lt * PAIR:lt * PAIR + wp, lt * PAIR:(lt + 1) * PAIR]
                     + bias_ref[hd] + maskbuf[2 * qi + lt])
                p, li = _softmax_cols(s, sink)
                pb = p.astype(BF16)
                pcols.append(jnp.concatenate([pb, zpad] if lt == 0 else [zpad, pb], axis=0))
                linv.append(li)
            pT = jnp.concatenate(pcols, axis=1)
            oT = jnp.dot(vhist[kvh, :, r0:r0 + w], pT, preferred_element_type=F32)
            o = oT * jnp.concatenate(linv, axis=1)
            aT[hd, :, r0:r0 + QUAD] = (o * gts[hd, :, r0:r0 + QUAD]).astype(BF16)
        return carry

    lax.fori_loop(0, N_HEADS, head, 0)

    yT = jnp.dot(woT_ref[...], aT[...].reshape(D_MODEL, tile), preferred_element_type=F32)
    y_ref[...] = x + yT.T

    khist[:, 0:hw, :] = khist[:, tile:tile + hw, :]
    vhist[:, :, 0:hw] = vhist[:, :, tile:tile + hw]


def _sample_layer_kernel(*refs, cfg):
    if cfg.use_sinks:
        sinks_ref, refs = refs[0], refs[1:]
    else:
        sinks_ref = None
    (x_ref, ck_ref, cv_ref, ng_ref, wk_ref, wqT_ref, wvT_ref, wgT_ref, woT_ref, gmat_ref,
     gq_ref, gk_ref, bias_ref, y_ref, newk_ref, newv_ref,
     kwin, vwin, qpad, gts, aT) = refs
    hw = cfg.hist
    tn = CHUNK

    x = x_ref[...]
    h = _rms_rows(x, ng_ref[...]).astype(BF16)

    kn = _project_k(h, wk_ref, gmat_ref, gk_ref, cfg)
    ck = ck_ref[...]
    newk_ref[0:hw - tn, :] = ck[tn:, :]
    newk_ref[hw - tn:, :] = kn
    for kb in range(cfg.kvd // LANES):
        kwin[kb, 0:hw, :] = ck[:, kb * LANES:(kb + 1) * LANES].astype(BF16)
        kwin[kb, hw:hw + tn, :] = kn[:, kb * LANES:(kb + 1) * LANES].astype(BF16)

    _project_q(h, wqT_ref, gq_ref, qpad, cfg)

    vT = lax.dot_general(wvT_ref[...], h, NT_DIMS, preferred_element_type=F32)
    cv = cv_ref[...]
    vwin[:, :, 0:hw] = cv.T.reshape(cfg.n_kv, HEAD_DIM, hw).astype(BF16)
    vwin[:, :, hw:hw + tn] = vT.reshape(cfg.n_kv, HEAD_DIM, tn).astype(BF16)
    newv_ref[0:hw - tn, :] = cv[tn:, :]
    vT_wide = jnp.concatenate([vT, jnp.zeros_like(vT)], axis=1)
    newv_ref[hw - tn:, :] = vT_wide.T[:tn, :]

    gT = lax.dot_general(wgT_ref[...], h, NT_DIMS, preferred_element_type=F32)
    gts[...] = (gT * jax.nn.sigmoid(gT)).reshape(N_HEADS, HEAD_DIM, tn)

    def head(hd, carry):
        kvh = hd // cfg.group
        kb = kvh // 2
        sink = sinks_ref[hd] if cfg.use_sinks else None
        sT = jnp.dot(kwin[kb], qpad[hd], preferred_element_type=F32)
        p, li = _softmax_cols(sT + bias_ref[hd], sink)
        oT = jnp.dot(vwin[kvh], p.astype(BF16), preferred_element_type=F32)
        aT[hd] = (oT * li * gts[hd]).astype(BF16)
        return carry

    lax.fori_loop(0, N_HEADS, head, 0)

    yT = jnp.dot(woT_ref[...], aT[...].reshape(D_MODEL, tn), preferred_element_type=F32)
    yT_wide = jnp.concatenate([yT, jnp.zeros_like(yT)], axis=1)
    y_ref[...] = x + yT_wide.T[:tn, :]


def _const_spec(shape):
    nd = len(shape)
    return pl.BlockSpec(shape, lambda i, _nd=nd: (0,) * _nd)


def _nbytes(shape, dtype):
    return math.prod(shape) * jnp.dtype(dtype).itemsize


def _vmem_limit(block_bytes, scratch_bytes, temp_bytes):
    est = 2 * block_bytes + scratch_bytes + temp_bytes
    return int(min(VMEM_BYTES - 4 * 2**20, max(est, 32 * 2**20)))


def _weight_operands(p):
    return [p["ng"], p["wk"], p["wqT"], p["wvT"], p["wgT"], p["woT"], p["gmat"], p["gq"], p["gk"]]


def _prompt_layer(x, p, cfg, tile):
    s = x.shape[0]
    hw, kvd = cfg.hist, cfg.kvd
    nq = tile // QUAD
    consts = _weight_operands(p) + [p["bias"]]
    operands = [x] + consts
    in_specs = [pl.BlockSpec((tile, D_MODEL), lambda t: (t, 0))] + [_const_spec(c.shape) for c in consts]
    if cfg.use_sinks:
        operands = [p["sinks"]] + operands
        in_specs = [pl.BlockSpec(memory_space=pltpu.SMEM)] + in_specs
    out_shape = (jax.ShapeDtypeStruct((s, D_MODEL), F32),
                 jax.ShapeDtypeStruct((hw, kvd), F32),
                 jax.ShapeDtypeStruct((hw, kvd), F32))
    out_specs = (pl.BlockSpec((tile, D_MODEL), lambda t: (t, 0)),
                 _const_spec((hw, kvd)), _const_spec((hw, kvd)))
    scratch = [
        ((kvd // LANES, hw + tile, LANES), BF16),
        ((cfg.n_kv, HEAD_DIM, hw + tile), BF16),
        ((N_HEADS, 2 * HEAD_DIM, tile), BF16),
        ((N_HEADS, HEAD_DIM, tile), F32),
        ((N_HEADS, HEAD_DIM, tile), BF16),
        ((2 * nq, cfg.pair_win, LANES), F32),
    ]
    block_bytes = (2 * _nbytes((tile, D_MODEL), F32) + 2 * _nbytes((hw, kvd), F32)
                   + sum(_nbytes(c.shape, c.dtype) for c in consts))
    scratch_bytes = sum(_nbytes(sh, dt) for sh, dt in scratch)
    temp_bytes = 6 * _nbytes((D_MODEL, tile), F32) + 4 * _nbytes((cfg.quad_win, QUAD), F32)
    return pl.pallas_call(
        functools.partial(_prompt_layer_kernel, cfg=cfg, tile=tile),
        grid=(s // tile,),
        in_specs=in_specs,
        out_specs=out_specs,
        out_shape=out_shape,
        scratch_shapes=[pltpu.VMEM(sh, dt) for sh, dt in scratch],
        compiler_params=pltpu.CompilerParams(
            dimension_semantics=("arbitrary",),
            vmem_limit_bytes=_vmem_limit(block_bytes, scratch_bytes, temp_bytes)),
        name=f"prompt_layer_kv{cfg.n_kv}",
    )(*operands)


def _sample_layer(x, ck, cv, p, cfg):
    b, tn, _ = x.shape
    hw, kvd = cfg.hist, cfg.kvd
    bias_s = p["bias"][:, :hw + tn, :tn]
    consts = _weight_operands(p) + [bias_s]
    operands = [x, ck, cv] + consts
    in_specs = ([pl.BlockSpec((None, tn, D_MODEL), lambda i: (i, 0, 0)),
                 pl.BlockSpec((None, hw, kvd), lambda i: (i, 0, 0)),
                 pl.BlockSpec((None, hw, kvd), lambda i: (i, 0, 0))]
                + [_const_spec(c.shape) for c in consts])
    if cfg.use_sinks:
        operands = [p["sinks"]] + operands
        in_specs = [pl.BlockSpec(memory_space=pltpu.SMEM)] + in_specs
    out_shape = (jax.ShapeDtypeStruct((b, tn, D_MODEL), F32),
                 jax.ShapeDtypeStruct((b, hw, kvd), F32),
                 jax.ShapeDtypeStruct((b, hw, kvd), F32))
    out_specs = (pl.BlockSpec((None, tn, D_MODEL), lambda i: (i, 0, 0)),
                 pl.BlockSpec((None, hw, kvd), lambda i: (i, 0, 0)),
                 pl.BlockSpec((None, hw, kvd), lambda i: (i, 0, 0)))
    scratch = [
        ((kvd // LANES, hw + tn, LANES), BF16),
        ((cfg.n_kv, HEAD_DIM, hw + tn), BF16),
        ((N_HEADS, 2 * HEAD_DIM, tn), BF16),
        ((N_HEADS, HEAD_DIM, tn), F32),
        ((N_HEADS, HEAD_DIM, tn), BF16),
    ]
    block_bytes = (2 * _nbytes((tn, D_MODEL), F32) + 4 * _nbytes((hw, kvd), F32)
                   + sum(_nbytes(c.shape, c.dtype) for c in consts))
    scratch_bytes = sum(_nbytes(sh, dt) for sh, dt in scratch)
    temp_bytes = 6 * _nbytes((D_MODEL, LANES), F32) + 3 * _nbytes((hw, kvd), F32)
    return pl.pallas_call(
        functools.partial(_sample_layer_kernel, cfg=cfg),
        grid=(b,),
        in_specs=in_specs,
        out_specs=out_specs,
        out_shape=out_shape,
        scratch_shapes=[pltpu.VMEM(sh, dt) for sh, dt in scratch],
        compiler_params=pltpu.CompilerParams(
            dimension_semantics=("arbitrary",),
            vmem_limit_bytes=_vmem_limit(block_bytes, scratch_bytes, temp_bytes)),
        name=f"sample_layer_kv{cfg.n_kv}",
    )(*operands)


def _t5_bucket(rel):
    nb = T5_BUCKETS // 2
    max_exact = nb // 2
    ret = jnp.where(rel > 0, nb, 0)
    n = jnp.abs(rel)
    nf = jnp.maximum(n, 1).astype(F32)
    large = max_exact + (jnp.log(nf / max_exact) / math.log(T5_MAX_DIST / max_exact)
                         * (nb - max_exact)).astype(jnp.int32)
    large = jnp.minimum(large, nb - 1)
    return ret + jnp.where(n < max_exact, n, large)


def _pair_bias(cfg, table_fn):
    hw, wp = cfg.hist, cfg.pair_win
    qpos = hw + jnp.arange(PAIR, dtype=jnp.int32)
    kpos = jnp.arange(wp, dtype=jnp.int32)
    b = table_fn(qpos[None, :], kpos[:, None]).astype(F32)
    qc = qpos // CHUNK
    kc = kpos // CHUNK
    valid = (kc[:, None] <= qc[None, :]) & (kc[:, None] >= qc[None, :] - cfg.past_chunks)
    return jnp.where(valid[None], b, NEG_INF)


def _prep_layer(cfg, norm_g, w_in, q_g, k_g, w_out, bias, sinks):
    wq, wkv = N_HEADS * HEAD_DIM, cfg.kvd
    gn = cfg.gnorm
    blk = jnp.arange(gn) // HEAD_DIM
    p = {
        "ng": norm_g[None, :],
        "wk": w_in[:, wq:wq + wkv].astype(BF16),
        "wqT": w_in[:, :wq].T.astype(BF16),
        "wvT": w_in[:, wq + wkv:wq + 2 * wkv].T.astype(BF16),
        "wgT": w_in[:, wq + 2 * wkv:].T.astype(BF16),
        "woT": w_out.T.astype(BF16),
        "gmat": (blk[:, None] == blk[None, :]).astype(BF16),
        "gq": jnp.broadcast_to((q_g * (HEAD_DIM ** -0.5))[:, None], (HEAD_DIM, LANES)).astype(F32),
        "gk": jnp.tile(k_g, cfg.n_kv)[None, :],
        "bias": bias,
    }
    if cfg.use_sinks:
        p["sinks"] = sinks.astype(F32)
    return p


def _prompt_tile(cfg):
    return QUAD


def kernel(x_prompt, x_sample, cache_k_a, cache_v_a, cache_k_b, cache_v_b, t5_table, norm_a, w_in_a, q_norm_a, k_norm_a, sinks_a, w_out_a, norm_b, w_in_b, q_norm_b, k_norm_b, rel_bias_b, w_out_b):
    bsz, seq, _ = x_prompt.shape
    assert bsz == 1 and seq % QUAD == 0
    db = x_sample.shape[0]

    def t5_fn(qpos, kpos):
        return jnp.transpose(t5_table[_t5_bucket(kpos - qpos)], (2, 0, 1))

    def rel_fn(qpos, kpos):
        idx = jnp.clip(qpos - kpos, -REL_CLIP, REL_CLIP) + REL_CLIP
        return rel_bias_b[:, idx]

    pa = _prep_layer(CFG_A, norm_a, w_in_a, q_norm_a, k_norm_a, w_out_a,
                     _pair_bias(CFG_A, t5_fn), sinks_a)
    pb = _prep_layer(CFG_B, norm_b, w_in_b, q_norm_b, k_norm_b, w_out_b,
                     _pair_bias(CFG_B, rel_fn), None)

    xp = x_prompt.reshape(seq, D_MODEL)
    yp_a, k_a_p, v_a_p = _prompt_layer(xp, pa, CFG_A, _prompt_tile(CFG_A))
    yp_b, k_b_p, v_b_p = _prompt_layer(yp_a, pb, CFG_B, _prompt_tile(CFG_B))

    ys_a, k_a_s, v_a_s = _sample_layer(
        x_sample, cache_k_a.reshape(db, CFG_A.hist, CFG_A.kvd),
        cache_v_a.reshape(db, CFG_A.hist, CFG_A.kvd), pa, CFG_A)
    ys_b, k_b_s, v_b_s = _sample_layer(
        ys_a, cache_k_b.reshape(db, CFG_B.hist, CFG_B.kvd),
        cache_v_b.reshape(db, CFG_B.hist, CFG_B.kvd), pb, CFG_B)

    return (yp_b.reshape(1, seq, D_MODEL), ys_b,
            k_a_p.reshape(1, CFG_A.hist, CFG_A.n_kv, HEAD_DIM),
            v_a_p.reshape(1, CFG_A.hist, CFG_A.n_kv, HEAD_DIM),
            k_b_p.reshape(1, CFG_B.hist, CFG_B.n_kv, HEAD_DIM),
            v_b_p.reshape(1, CFG_B.hist, CFG_B.n_kv, HEAD_DIM),
            k_a_s.reshape(db, CFG_A.hist, CFG_A.n_kv, HEAD_DIM),
            v_a_s.reshape(db, CFG_A.hist, CFG_A.n_kv, HEAD_DIM),
            k_b_s.reshape(db, CFG_B.hist, CFG_B.n_kv, HEAD_DIM),
            v_b_s.reshape(db, CFG_B.hist, CFG_B.n_kv, HEAD_DIM))
```

```python
import functools
import math
from typing import NamedTuple

import jax
import jax.numpy as jnp
import numpy as np
from jax import lax
from jax.experimental import pallas as pl
from jax.experimental.pallas import tpu as pltpu

D_MODEL = 1024
CHUNK = 64
HEAD_DIM = 64
N_HEADS = 16
REL_CLIP = 128
T5_BUCKETS = 32
T5_MAX_DIST = 128
EPS = 1e-6
NEG_INF = -1e30

LANES = 128
QUAD = 256
PAIR = LANES
MXU_DIM = 256
VMEM_BYTES = 64 * 2**20

BF16 = jnp.bfloat16
F32 = jnp.float32
NT_DIMS = (((1,), (1,)), ((), ()))


class LayerCfg(NamedTuple):
    n_kv: int
    past_chunks: int
    use_sinks: bool

    @property
    def kvd(self):
        return self.n_kv * HEAD_DIM

    @property
    def group(self):
        return N_HEADS // self.n_kv

    @property
    def hist(self):
        return self.past_chunks * CHUNK

    @property
    def pair_win(self):
        return self.hist + PAIR

    @property
    def quad_win(self):
        return self.hist + QUAD

    @property
    def gnorm(self):
        return min(self.kvd, MXU_DIM)


CFG_A = LayerCfg(n_kv=2, past_chunks=2, use_sinks=True)
CFG_B = LayerCfg(n_kv=16, past_chunks=8, use_sinks=False)


def _rms_rows(x, g_row):
    ms = jnp.mean(x * x, axis=-1, keepdims=True)
    return x * lax.rsqrt(ms + EPS) * g_row


def _project_k(h, wk_ref, gmat_ref, gk_ref, cfg):
    k = jnp.dot(h, wk_ref[...], preferred_element_type=F32)
    ksq = (k * k).astype(BF16)
    gm = gmat_ref[...]
    gn = cfg.gnorm
    ss = jnp.concatenate(
        [jnp.dot(ksq[:, c:c + gn], gm, preferred_element_type=F32)
         for c in range(0, cfg.kvd, gn)], axis=1)
    return k * lax.rsqrt(ss * (1.0 / HEAD_DIM) + EPS) * gk_ref[...]


def _project_q(h, wqT_ref, gq_ref, qpad_ref, cfg):
    t = h.shape[0]
    qT = lax.dot_general(wqT_ref[...], h, NT_DIMS, preferred_element_type=F32)
    q3 = qT.reshape(N_HEADS, HEAD_DIM, t)
    ms = jnp.mean(q3 * q3, axis=1, keepdims=True)
    g = gq_ref[...][:, :t] if t < LANES else jnp.concatenate([gq_ref[...]] * (t // LANES), axis=1)
    qn = (q3 * lax.rsqrt(ms + EPS) * g[None]).astype(BF16)
    z = jnp.zeros((HEAD_DIM, t), BF16)
    for hd in range(N_HEADS):
        odd = (hd // cfg.group) % 2
        qpad_ref[hd] = jnp.concatenate([z, qn[hd]] if odd else [qn[hd], z], axis=0)


def _softmax_cols(s, sink):
    m = jnp.max(s, axis=0, keepdims=True)
    if sink is not None:
        m = jnp.maximum(m, sink)
    p = jnp.exp(s - m)
    l = jnp.sum(p, axis=0, keepdims=True)
    if sink is not None:
        l = l + jnp.exp(sink - m)
    return p, 1.0 / l


def _prompt_layer_kernel(*refs, cfg, tile):
    if cfg.use_sinks:
        sinks_ref, refs = refs[0], refs[1:]
    else:
        sinks_ref = None
    (x_ref, ng_ref, wk_ref, wqT_ref, wvT_ref, wgT_ref, woT_ref, gmat_ref, gq_ref, gk_ref,
     bias_ref, y_ref, kkeep_ref, vkeep_ref,
     khist, vhist, qpad, gts, aT, maskbuf, sbuf, pbuf, lbuf) = refs
    t = pl.program_id(0)
    nt = pl.num_programs(0)
    hw, wp, w = cfg.hist, cfg.pair_win, cfg.quad_win
    nq = tile // QUAD

    @pl.when(t == 0)
    def _zero_history():
        khist[:, :hw, :] = jnp.zeros((khist.shape[0], hw, LANES), BF16)
        vhist[:, :, :hw] = jnp.zeros((cfg.n_kv, HEAD_DIM, hw), BF16)

    @pl.when(t > 0)
    def _roll_history():
        khist[:, 0:hw, :] = khist[:, tile:tile + hw, :]
        vhist[:, :, 0:hw] = vhist[:, :, tile:tile + hw]

    n_invalid = jnp.maximum(hw - t * tile, 0)
    rows = lax.broadcasted_iota(jnp.int32, (wp, LANES), 0)
    for qi in range(nq):
        for lt in range(2):
            off = qi * QUAD + lt * PAIR
            maskbuf[2 * qi + lt] = jnp.where(rows + off < n_invalid, NEG_INF, 0.0).astype(F32)

    x = x_ref[...]
    h = _rms_rows(x, ng_ref[...]).astype(BF16)

    kn = _project_k(h, wk_ref, gmat_ref, gk_ref, cfg)
    for kb in range(cfg.kvd // LANES):
        khist[kb, hw:hw + tile, :] = kn[:, kb * LANES:(kb + 1) * LANES].astype(BF16)

    _project_q(h, wqT_ref, gq_ref, qpad, cfg)

    vT = lax.dot_general(wvT_ref[...], h, NT_DIMS, preferred_element_type=F32)
    vhist[:, :, hw:hw + tile] = vT.reshape(cfg.n_kv, HEAD_DIM, tile).astype(BF16)

    gT = lax.dot_general(wgT_ref[...], h, NT_DIMS, preferred_element_type=F32)
    gts[...] = (gT * jax.nn.sigmoid(gT)).reshape(N_HEADS, HEAD_DIM, tile)

    if tile >= hw:
        @pl.when(t == nt - 1)
        def _keep():
            kkeep_ref[...] = kn[tile - hw:, :]
            vkeep_ref[...] = vT[:, tile - hw:].T
    else:
        @pl.when(t >= nt - hw // tile)
        def _keep():
            row0 = pl.multiple_of(hw - (nt - t) * tile, tile)
            kkeep_ref[pl.ds(row0, tile), :] = kn
            vkeep_ref[pl.ds(row0, tile), :] = vT.T

    @pl.when(t == 0)
    def _zero_prob_corners():
        zpad = jnp.zeros((2, nq, PAIR, LANES), BF16)
        pbuf[:, :, wp:w, 0:PAIR] = zpad
        pbuf[:, :, 0:PAIR, PAIR:QUAD] = zpad

    def scores(hd, slot):
        kb = (hd // cfg.group) // 2
        for qi in range(nq):
            r0 = qi * QUAD
            sbuf[slot, qi] = jnp.dot(khist[kb, r0:r0 + w, :], qpad[hd, :, r0:r0 + QUAD],
                                     preferred_element_type=F32)

    def softmax(hd, slot):
        sink = sinks_ref[hd] if cfg.use_sinks else None
        for qi in range(nq):
            for lt in range(2):
                lanes = slice(lt * PAIR, (lt + 1) * PAIR)
                keys = slice(lt * PAIR, lt * PAIR + wp)
                s = sbuf[slot, qi, keys, lanes] + bias_ref[hd] + maskbuf[2 * qi + lt]
                p, li = _softmax_cols(s, sink)
                pbuf[slot, qi, keys, lanes] = p.astype(BF16)
                lbuf[slot, qi, :, lanes] = li

    def weighted_values(hd, slot):
        kvh = hd // cfg.group
        for qi in range(nq):
            r0 = qi * QUAD
            oT = jnp.dot(vhist[kvh, :, r0:r0 + w], pbuf[slot, qi], preferred_element_type=F32)
            o = oT * lbuf[slot, qi]
            aT[hd, :, r0:r0 + QUAD] = (o * gts[hd, :, r0:r0 + QUAD]).astype(BF16)

    scores(0, 0)
    scores(1, 1)
    softmax(0, 0)

    def head_pair(j, carry):
        odd = 2 * j + 1
        scores(odd + 1, 0)
        weighted_values(odd - 1, 0)
        softmax(odd, 1)
        scores(odd + 2, 1)
        weighted_values(odd, 1)
        softmax(odd + 1, 0)
        return carry

    lax.fori_loop(0, N_HEADS // 2 - 1, head_pair, 0)
    weighted_values(N_HEADS - 2, 0)
    softmax(N_HEADS - 1, 1)
    weighted_values(N_HEADS - 1, 1)

    yT = jnp.dot(woT_ref[...], aT[...].reshape(D_MODEL, tile), preferred_element_type=F32)
    y_ref[...] = x + yT.T


def _sample_layer_kernel(*refs, cfg):
    if cfg.use_sinks:
        sinks_ref, refs = refs[0], refs[1:]
    else:
        sinks_ref = None
    (x_ref, ck_ref, cv_ref, ng_ref, wk_ref, wqT_ref, wvT_ref, wgT_ref, woT_ref, gmat_ref,
     gq_ref, gk_ref, bias_ref, y_ref, newk_ref, newv_ref,
     kwin, vwin, qpad, gts, aT) = refs
    hw = cfg.hist
    tn = CHUNK

    x = x_ref[...]
    h = _rms_rows(x, ng_ref[...]).astype(BF16)

    kn = _project_k(h, wk_ref, gmat_ref, gk_ref, cfg)
    ck = ck_ref[...]
    newk_ref[0:hw - tn, :] = ck[tn:, :]
    newk_ref[hw - tn:, :] = kn
    for kb in range(cfg.kvd // LANES):
        kwin[kb, 0:hw, :] = ck[:, kb * LANES:(kb + 1) * LANES].astype(BF16)
        kwin[kb, hw:hw + tn, :] = kn[:, kb * LANES:(kb + 1) * LANES].astype(BF16)

    _project_q(h, wqT_ref, gq_ref, qpad, cfg)

    vT = lax.dot_general(wvT_ref[...], h, NT_DIMS, preferred_element_type=F32)
    cv = cv_ref[...]
    vwin[:, :, 0:hw] = cv.T.reshape(cfg.n_kv, HEAD_DIM, hw).astype(BF16)
    vwin[:, :, hw:hw + tn] = vT.reshape(cfg.n_kv, HEAD_DIM, tn).astype(BF16)
    newv_ref[0:hw - tn, :] = cv[tn:, :]
    vT_wide = jnp.concatenate([vT, jnp.zeros_like(vT)], axis=1)
    newv_ref[hw - tn:, :] = vT_wide.T[:tn, :]

    gT = lax.dot_general(wgT_ref[...], h, NT_DIMS, preferred_element_type=F32)
    gts[...] = (gT * jax.nn.sigmoid(gT)).reshape(N_HEADS, HEAD_DIM, tn)

    def head(hd, carry):
        kvh = hd // cfg.group
        kb = kvh // 2
        sink = sinks_ref[hd] if cfg.use_sinks else None
        sT = jnp.dot(kwin[kb], qpad[hd], preferred_element_type=F32)
        p, li = _softmax_cols(sT + bias_ref[hd], sink)
        oT = jnp.dot(vwin[kvh], p.astype(BF16), preferred_element_type=F32)
        aT[hd] = (oT * li * gts[hd]).astype(BF16)
        return carry

    lax.fori_loop(0, N_HEADS, head, 0)

    yT = jnp.dot(woT_ref[...], aT[...].reshape(D_MODEL, tn), preferred_element_type=F32)
    yT_wide = jnp.concatenate([yT, jnp.zeros_like(yT)], axis=1)
    y_ref[...] = x + yT_wide.T[:tn, :]


def _const_spec(shape):
    nd = len(shape)
    return pl.BlockSpec(shape, lambda i, _nd=nd: (0,) * _nd)


def _nbytes(shape, dtype):
    return math.prod(shape) * jnp.dtype(dtype).itemsize


def _vmem_limit(block_bytes, scratch_bytes, temp_bytes):
    est = 2 * block_bytes + scratch_bytes + temp_bytes
    return int(min(VMEM_BYTES - 4 * 2**20, max(est, 32 * 2**20)))


def _weight_operands(p):
    return [p["ng"], p["wk"], p["wqT"], p["wvT"], p["wgT"], p["woT"], p["gmat"], p["gq"], p["gk"]]


def _prompt_layer(x, p, cfg, tile):
    s = x.shape[0]
    hw, kvd = cfg.hist, cfg.kvd
    nq = tile // QUAD
    consts = _weight_operands(p) + [p["bias"]]
    operands = [x] + consts
    in_specs = [pl.BlockSpec((tile, D_MODEL), lambda t: (t, 0))] + [_const_spec(c.shape) for c in consts]
    if cfg.use_sinks:
        operands = [p["sinks"]] + operands
        in_specs = [pl.BlockSpec(memory_space=pltpu.SMEM)] + in_specs
    out_shape = (jax.ShapeDtypeStruct((s, D_MODEL), F32),
                 jax.ShapeDtypeStruct((hw, kvd), F32),
                 jax.ShapeDtypeStruct((hw, kvd), F32))
    out_specs = (pl.BlockSpec((tile, D_MODEL), lambda t: (t, 0)),
                 _const_spec((hw, kvd)), _const_spec((hw, kvd)))
    scratch = [
        ((kvd // LANES, hw + tile, LANES), BF16),
        ((cfg.n_kv, HEAD_DIM, hw + tile), BF16),
        ((N_HEADS, 2 * HEAD_DIM, tile), BF16),
        ((N_HEADS, HEAD_DIM, tile), F32),
        ((N_HEADS, HEAD_DIM, tile), BF16),
        ((2 * nq, cfg.pair_win, LANES), F32),
        ((2, nq, cfg.quad_win, QUAD), F32),
        ((2, nq, cfg.quad_win, QUAD), BF16),
        ((2, nq, 1, QUAD), F32),
    ]
    block_bytes = (2 * _nbytes((tile, D_MODEL), F32) + 2 * _nbytes((hw, kvd), F32)
                   + sum(_nbytes(c.shape, c.dtype) for c in consts))
    scratch_bytes = sum(_nbytes(sh, dt) for sh, dt in scratch)
    temp_bytes = 6 * _nbytes((D_MODEL, tile), F32) + 4 * _nbytes((cfg.quad_win, QUAD), F32)
    return pl.pallas_call(
        functools.partial(_prompt_layer_kernel, cfg=cfg, tile=tile),
        grid=(s // tile,),
        in_specs=in_specs,
        out_specs=out_specs,
        out_shape=out_shape,
        scratch_shapes=[pltpu.VMEM(sh, dt) for sh, dt in scratch],
        compiler_params=pltpu.CompilerParams(
            dimension_semantics=("arbitrary",),
            vmem_limit_bytes=_vmem_limit(block_bytes, scratch_bytes, temp_bytes)),
        name=f"prompt_layer_kv{cfg.n_kv}",
    )(*operands)


def _sample_layer(x, ck, cv, p, cfg):
    b, tn, _ = x.shape
    hw, kvd = cfg.hist, cfg.kvd
    bias_s = p["bias"][:, :hw + tn, :tn]
    consts = _weight_operands(p) + [bias_s]
    operands = [x, ck, cv] + consts
    in_specs = ([pl.BlockSpec((None, tn, D_MODEL), lambda i: (i, 0, 0)),
                 pl.BlockSpec((None, hw, kvd), lambda i: (i, 0, 0)),
                 pl.BlockSpec((None, hw, kvd), lambda i: (i, 0, 0))]
                + [_const_spec(c.shape) for c in consts])
    if cfg.use_sinks:
        operands = [p["sinks"]] + operands
        in_specs = [pl.BlockSpec(memory_space=pltpu.SMEM)] + in_specs
    out_shape = (jax.ShapeDtypeStruct((b, tn, D_MODEL), F32),
                 jax.ShapeDtypeStruct((b, hw, kvd), F32),
                 jax.ShapeDtypeStruct((b, hw, kvd), F32))
    out_specs = (pl.BlockSpec((None, tn, D_MODEL), lambda i: (i, 0, 0)),
                 pl.BlockSpec((None, hw, kvd), lambda i: (i, 0, 0)),
                 pl.BlockSpec((None, hw, kvd), lambda i: (i, 0, 0)))
    scratch = [
        ((kvd // LANES, hw + tn, LANES), BF16),
        ((cfg.n_kv, HEAD_DIM, hw + tn), BF16),
        ((N_HEADS, 2 * HEAD_DIM, tn), BF16),
        ((N_HEADS, HEAD_DIM, tn), F32),
        ((N_HEADS, HEAD_DIM, tn), BF16),
    ]
    block_bytes = (2 * _nbytes((tn, D_MODEL), F32) + 4 * _nbytes((hw, kvd), F32)
                   + sum(_nbytes(c.shape, c.dtype) for c in consts))
    scratch_bytes = sum(_nbytes(sh, dt) for sh, dt in scratch)
    temp_bytes = 6 * _nbytes((D_MODEL, LANES), F32) + 3 * _nbytes((hw, kvd), F32)
    return pl.pallas_call(
        functools.partial(_sample_layer_kernel, cfg=cfg),
        grid=(b,),
        in_specs=in_specs,
        out_specs=out_specs,
        out_shape=out_shape,
        scratch_shapes=[pltpu.VMEM(sh, dt) for sh, dt in scratch],
        compiler_params=pltpu.CompilerParams(
            dimension_semantics=("arbitrary",),
            vmem_limit_bytes=_vmem_limit(block_bytes, scratch_bytes, temp_bytes)),
        name=f"sample_layer_kv{cfg.n_kv}",
    )(*operands)


def _t5_bucket(rel):
    nb = T5_BUCKETS // 2
    max_exact = nb // 2
    ret = jnp.where(rel > 0, nb, 0)
    n = jnp.abs(rel)
    nf = jnp.maximum(n, 1).astype(F32)
    large = max_exact + (jnp.log(nf / max_exact) / math.log(T5_MAX_DIST / max_exact)
                         * (nb - max_exact)).astype(jnp.int32)
    large = jnp.minimum(large, nb - 1)
    return ret + jnp.where(n < max_exact, n, large)


def _pair_bias(cfg, bias_of_offset):
    hw, wp = cfg.hist, cfg.pair_win
    period = wp + PAIR
    idx = np.arange(period)
    offset = np.where(idx < PAIR, hw + idx, hw + idx - period)
    gen = bias_of_offset(jnp.asarray(offset, jnp.int32)).astype(F32)
    flat = jnp.tile(gen, (1, wp))[:, :wp * (period - 1)]
    b = flat.reshape(N_HEADS, wp, period - 1)[:, :, :PAIR]
    qc = (hw + np.arange(PAIR)) // CHUNK
    kc = np.arange(wp) // CHUNK
    valid = (kc[:, None] <= qc[None, :]) & (kc[:, None] >= qc[None, :] - cfg.past_chunks)
    return jnp.where(jnp.asarray(valid)[None], b, NEG_INF)


def _prep_layer(cfg, norm_g, w_in, q_g, k_g, w_out, bias, sinks):
    wq, wkv = N_HEADS * HEAD_DIM, cfg.kvd
    gn = cfg.gnorm
    blk = jnp.arange(gn) // HEAD_DIM
    p = {
        "ng": norm_g[None, :],
        "wk": w_in[:, wq:wq + wkv].astype(BF16),
        "wqT": w_in[:, :wq].T.astype(BF16),
        "wvT": w_in[:, wq + wkv:wq + 2 * wkv].T.astype(BF16),
        "wgT": w_in[:, wq + 2 * wkv:].T.astype(BF16),
        "woT": w_out.T.astype(BF16),
        "gmat": (blk[:, None] == blk[None, :]).astype(BF16),
        "gq": jnp.broadcast_to((q_g * (HEAD_DIM ** -0.5))[:, None], (HEAD_DIM, LANES)).astype(F32),
        "gk": jnp.tile(k_g, cfg.n_kv)[None, :],
        "bias": bias,
    }
    if cfg.use_sinks:
        p["sinks"] = sinks.astype(F32)
    return p


def _prompt_tile(cfg):
    return QUAD


def kernel(x_prompt, x_sample, cache_k_a, cache_v_a, cache_k_b, cache_v_b, t5_table, norm_a, w_in_a, q_norm_a, k_norm_a, sinks_a, w_out_a, norm_b, w_in_b, q_norm_b, k_norm_b, rel_bias_b, w_out_b):
    bsz, seq, _ = x_prompt.shape
    assert bsz == 1 and seq % QUAD == 0
    db = x_sample.shape[0]

    def t5_fn(offset):
        return t5_table[_t5_bucket(-offset)].T

    def rel_fn(offset):
        return rel_bias_b[:, jnp.clip(offset, -REL_CLIP, REL_CLIP) + REL_CLIP]

    pa = _prep_layer(CFG_A, norm_a, w_in_a, q_norm_a, k_norm_a, w_out_a,
                     _pair_bias(CFG_A, t5_fn), sinks_a)
    pb = _prep_layer(CFG_B, norm_b, w_in_b, q_norm_b, k_norm_b, w_out_b,
                     _pair_bias(CFG_B, rel_fn), None)

    xp = x_prompt.reshape(seq, D_MODEL)
    yp_a, k_a_p, v_a_p = _prompt_layer(xp, pa, CFG_A, _prompt_tile(CFG_A))
    yp_b, k_b_p, v_b_p = _prompt_layer(yp_a, pb, CFG_B, _prompt_tile(CFG_B))

    ys_a, k_a_s, v_a_s = _sample_layer(
        x_sample, cache_k_a.reshape(db, CFG_A.hist, CFG_A.kvd),
        cache_v_a.reshape(db, CFG_A.hist, CFG_A.kvd), pa, CFG_A)
    ys_b, k_b_s, v_b_s = _sample_layer(
        ys_a, cache_k_b.reshape(db, CFG_B.hist, CFG_B.kvd),
        cache_v_b.reshape(db, CFG_B.hist, CFG_B.kvd), pb, CFG_B)

    return (yp_b.reshape(1, seq, D_MODEL), ys_b,
            k_a_p.reshape(1, CFG_A.hist, CFG_A.n_kv, HEAD_DIM),
            v_a_p.reshape(1, CFG_A.hist, CFG_A.n_kv, HEAD_DIM),
            k_b_p.reshape(1, CFG_B.hist, CFG_B.n_kv, HEAD_DIM),
            v_b_p.reshape(1, CFG_B.hist, CFG_B.n_kv, HEAD_DIM),
            k_a_s.reshape(db, CFG_A.hist, CFG_A.n_kv, HEAD_DIM),
            v_a_s.reshape(db, CFG_A.hist, CFG_A.n_kv, HEAD_DIM),
            k_b_s.reshape(db, CFG_B.hist, CFG_B.n_kv, HEAD_DIM),
            v_b_s.reshape(db, CFG_B.hist, CFG_B.n_kv, HEAD_DIM))
```

```python
import functools
import math
from typing import NamedTuple

import jax
import jax.numpy as jnp
import numpy as np
from jax import lax
from jax.experimental import pallas as pl
from jax.experimental.pallas import tpu as pltpu

D_MODEL = 1024
CHUNK = 64
HEAD_DIM = 64
N_HEADS = 16
REL_CLIP = 128
T5_BUCKETS = 32
T5_MAX_DIST = 128
EPS = 1e-6
NEG_INF = -1e30

LANES = 128
QUAD = 256
PAIR = LANES
SAMPLE_GROUP = QUAD // CHUNK
MXU_DIM = 256
VMEM_BYTES = 64 * 2**20

BF16 = jnp.bfloat16
F32 = jnp.float32
NT_DIMS = (((1,), (1,)), ((), ()))


class LayerCfg(NamedTuple):
    n_kv: int
    past_chunks: int
    use_sinks: bool

    @property
    def kvd(self):
        return self.n_kv * HEAD_DIM

    @property
    def group(self):
        return N_HEADS // self.n_kv

    @property
    def hist(self):
        return self.past_chunks * CHUNK

    @property
    def pair_win(self):
        return self.hist + PAIR

    @property
    def quad_win(self):
        return self.hist + QUAD

    @property
    def gnorm(self):
        return min(self.kvd, MXU_DIM)


CFG_A = LayerCfg(n_kv=2, past_chunks=2, use_sinks=True)
CFG_B = LayerCfg(n_kv=16, past_chunks=8, use_sinks=False)


def _rms_rows(x, g_row):
    ms = jnp.mean(x * x, axis=-1, keepdims=True)
    return x * lax.rsqrt(ms + EPS) * g_row


def _project_k(h, wk_ref, gmat_ref, gk_ref, cfg):
    k = jnp.dot(h, wk_ref[...], preferred_element_type=F32)
    ksq = (k * k).astype(BF16)
    gm = gmat_ref[...]
    gn = cfg.gnorm
    ss = jnp.concatenate(
        [jnp.dot(ksq[:, c:c + gn], gm, preferred_element_type=F32)
         for c in range(0, cfg.kvd, gn)], axis=1)
    return k * lax.rsqrt(ss * (1.0 / HEAD_DIM) + EPS) * gk_ref[...]


def _project_q(h, wqT_ref, gq_ref, qpad_ref, cfg):
    t = h.shape[0]
    qT = lax.dot_general(wqT_ref[...], h, NT_DIMS, preferred_element_type=F32)
    q3 = qT.reshape(N_HEADS, HEAD_DIM, t)
    ms = jnp.mean(q3 * q3, axis=1, keepdims=True)
    g = gq_ref[...][:, :t] if t < LANES else jnp.concatenate([gq_ref[...]] * (t // LANES), axis=1)
    qn = (q3 * lax.rsqrt(ms + EPS) * g[None]).astype(BF16)
    z = jnp.zeros((HEAD_DIM, t), BF16)
    for hd in range(N_HEADS):
        odd = (hd // cfg.group) % 2
        qpad_ref[hd] = jnp.concatenate([z, qn[hd]] if odd else [qn[hd], z], axis=0)


def _softmax_cols(s, sink):
    m = jnp.max(s, axis=0, keepdims=True)
    if sink is not None:
        m = jnp.maximum(m, sink)
    p = jnp.exp(s - m)
    l = jnp.sum(p, axis=0, keepdims=True)
    if sink is not None:
        l = l + jnp.exp(sink - m)
    return p, 1.0 / l


def _pipelined_heads(scores, softmax, weighted_values):
    scores(0, 0)
    scores(1, 1)
    softmax(0, 0)

    def head_pair(j, carry):
        odd = 2 * j + 1
        scores(odd + 1, 0)
        weighted_values(odd - 1, 0)
        softmax(odd, 1)
        scores(odd + 2, 1)
        weighted_values(odd, 1)
        softmax(odd + 1, 0)
        return carry

    lax.fori_loop(0, N_HEADS // 2 - 1, head_pair, 0)
    weighted_values(N_HEADS - 2, 0)
    softmax(N_HEADS - 1, 1)
    weighted_values(N_HEADS - 1, 1)


def _prompt_layer_kernel(*refs, cfg, tile):
    if cfg.use_sinks:
        sinks_ref, refs = refs[0], refs[1:]
    else:
        sinks_ref = None
    (x_ref, ng_ref, wk_ref, wqT_ref, wvT_ref, wgT_ref, woT_ref, gmat_ref, gq_ref, gk_ref,
     bias_ref, y_ref, kkeep_ref, vkeep_ref,
     khist, vhist, qpad, gts, aT, maskbuf, sbuf, pbuf, lbuf) = refs
    t = pl.program_id(0)
    nt = pl.num_programs(0)
    hw, wp, w = cfg.hist, cfg.pair_win, cfg.quad_win
    nq = tile // QUAD

    @pl.when(t == 0)
    def _zero_history():
        khist[:, :hw, :] = jnp.zeros((khist.shape[0], hw, LANES), BF16)
        vhist[:, :, :hw] = jnp.zeros((cfg.n_kv, HEAD_DIM, hw), BF16)

    @pl.when(t > 0)
    def _roll_history():
        khist[:, 0:hw, :] = khist[:, tile:tile + hw, :]
        vhist[:, :, 0:hw] = vhist[:, :, tile:tile + hw]

    n_invalid = jnp.maximum(hw - t * tile, 0)
    rows = lax.broadcasted_iota(jnp.int32, (wp, LANES), 0)
    for qi in range(nq):
        for lt in range(2):
            off = qi * QUAD + lt * PAIR
            maskbuf[2 * qi + lt] = jnp.where(rows + off < n_invalid, NEG_INF, 0.0).astype(F32)

    x = x_ref[...]
    h = _rms_rows(x, ng_ref[...]).astype(BF16)

    kn = _project_k(h, wk_ref, gmat_ref, gk_ref, cfg)
    for kb in range(cfg.kvd // LANES):
        khist[kb, hw:hw + tile, :] = kn[:, kb * LANES:(kb + 1) * LANES].astype(BF16)

    _project_q(h, wqT_ref, gq_ref, qpad, cfg)

    vT = lax.dot_general(wvT_ref[...], h, NT_DIMS, preferred_element_type=F32)
    vhist[:, :, hw:hw + tile] = vT.reshape(cfg.n_kv, HEAD_DIM, tile).astype(BF16)

    gT = lax.dot_general(wgT_ref[...], h, NT_DIMS, preferred_element_type=F32)
    gts[...] = (gT * jax.nn.sigmoid(gT)).reshape(N_HEADS, HEAD_DIM, tile)

    if tile >= hw:
        @pl.when(t == nt - 1)
        def _keep():
            kkeep_ref[...] = kn[tile - hw:, :]
            vkeep_ref[...] = vT[:, tile - hw:].T
    else:
        @pl.when(t >= nt - hw // tile)
        def _keep():
            row0 = pl.multiple_of(hw - (nt - t) * tile, tile)
            kkeep_ref[pl.ds(row0, tile), :] = kn
            vkeep_ref[pl.ds(row0, tile), :] = vT.T

    @pl.when(t == 0)
    def _zero_prob_corners():
        zpad = jnp.zeros((2, nq, PAIR, LANES), BF16)
        pbuf[:, :, wp:w, 0:PAIR] = zpad
        pbuf[:, :, 0:PAIR, PAIR:QUAD] = zpad

    def scores(hd, slot):
        kb = (hd // cfg.group) // 2
        for qi in range(nq):
            r0 = qi * QUAD
            sbuf[slot, qi] = jnp.dot(khist[kb, r0:r0 + w, :], qpad[hd, :, r0:r0 + QUAD],
                                     preferred_element_type=F32)

    def softmax(hd, slot):
        sink = sinks_ref[hd] if cfg.use_sinks else None
        for qi in range(nq):
            for lt in range(2):
                lanes = slice(lt * PAIR, (lt + 1) * PAIR)
                keys = slice(lt * PAIR, lt * PAIR + wp)
                s = sbuf[slot, qi, keys, lanes] + bias_ref[hd] + maskbuf[2 * qi + lt]
                p, li = _softmax_cols(s, sink)
                pbuf[slot, qi, keys, lanes] = p.astype(BF16)
                lbuf[slot, qi, :, lanes] = li

    def weighted_values(hd, slot):
        kvh = hd // cfg.group
        for qi in range(nq):
            r0 = qi * QUAD
            oT = jnp.dot(vhist[kvh, :, r0:r0 + w], pbuf[slot, qi], preferred_element_type=F32)
            o = oT * lbuf[slot, qi]
            aT[hd, :, r0:r0 + QUAD] = (o * gts[hd, :, r0:r0 + QUAD]).astype(BF16)

    _pipelined_heads(scores, softmax, weighted_values)

    yT = jnp.dot(woT_ref[...], aT[...].reshape(D_MODEL, tile), preferred_element_type=F32)
    y_ref[...] = x + yT.T


def _sample_layer_kernel(*refs, cfg):
    if cfg.use_sinks:
        sinks_ref, refs = refs[0], refs[1:]
    else:
        sinks_ref = None
    (x_ref, ck_ref, cv_ref, ng_ref, wk_ref, wqT_ref, wvT_ref, wgT_ref, woT_ref, gmat_ref,
     gq_ref, gk_ref, bias_ref, y_ref, newk_ref, newv_ref,
     kwin, vwin, knew, vnew, qpad, gts, aT, sbuf, pbuf, lbuf) = refs
    b = pl.program_id(1)
    hw, kvd = cfg.hist, cfg.kvd
    tn, ns = CHUNK, SAMPLE_GROUP
    ts, ws = ns * tn, hw + tn

    @pl.when(b == 0)
    def _project():
        x = x_ref[...].reshape(ts, D_MODEL)
        h = _rms_rows(x, ng_ref[...]).astype(BF16)
        knew[...] = _project_k(h, wk_ref, gmat_ref, gk_ref, cfg).reshape(ns, tn, kvd)
        _project_q(h, wqT_ref, gq_ref, qpad, cfg)
        vT = lax.dot_general(wvT_ref[...], h, NT_DIMS, preferred_element_type=F32)
        vnew[...] = vT.T.reshape(ns, tn, kvd)
        gT = lax.dot_general(wgT_ref[...], h, NT_DIMS, preferred_element_type=F32)
        gts[...] = (gT * jax.nn.sigmoid(gT)).reshape(N_HEADS, HEAD_DIM, ts)

    ck = ck_ref[...]
    cv = cv_ref[...]
    kn = knew[b]
    vn = vnew[b]
    newk_ref[0:hw - tn, :] = ck[tn:, :]
    newk_ref[hw - tn:, :] = kn
    newv_ref[0:hw - tn, :] = cv[tn:, :]
    newv_ref[hw - tn:, :] = vn
    for kb in range(kvd // LANES):
        kwin[b, kb, 0:hw, :] = ck[:, kb * LANES:(kb + 1) * LANES].astype(BF16)
        kwin[b, kb, hw:ws, :] = kn[:, kb * LANES:(kb + 1) * LANES].astype(BF16)
    vrows = jnp.concatenate([cv, vn, jnp.zeros((LANES - tn, kvd), F32)], axis=0)
    vwin[b] = vrows.T[:, :ws].reshape(cfg.n_kv, HEAD_DIM, ws).astype(BF16)

    @pl.when(b == ns - 1)
    def _attend():
        wlane = lax.broadcasted_iota(jnp.int32, (2 * HEAD_DIM, ts), 1) // tn
        wmask = [(wlane == i).astype(F32).astype(BF16) for i in range(ns)]
        olane = lax.broadcasted_iota(jnp.int32, (HEAD_DIM, ts), 1) // tn

        def scores(hd, slot):
            kb = (hd // cfg.group) // 2
            q = qpad[hd]
            wbd = jnp.concatenate([q * wmask[i] for i in range(ns)], axis=0)
            kcat = jnp.concatenate([kwin[i, kb] for i in range(ns)], axis=1)
            sbuf[slot] = jnp.dot(kcat, wbd, preferred_element_type=F32)

        def softmax(hd, slot):
            sink = sinks_ref[hd] if cfg.use_sinks else None
            for lt in range(ts // LANES):
                lanes = slice(lt * LANES, (lt + 1) * LANES)
                p, li = _softmax_cols(sbuf[slot, :, lanes] + bias_ref[hd], sink)
                pbuf[slot, :, lanes] = p.astype(BF16)
                lbuf[slot, :, lanes] = li

        def weighted_values(hd, slot):
            kvh = hd // cfg.group
            vcat = jnp.concatenate([vwin[i, kvh] for i in range(ns)], axis=0)
            r = jnp.dot(vcat, pbuf[slot], preferred_element_type=F32)
            o = r[0:HEAD_DIM]
            for i in range(1, ns):
                o = jnp.where(olane == i, r[i * HEAD_DIM:(i + 1) * HEAD_DIM], o)
            aT[hd] = (o * lbuf[slot] * gts[hd]).astype(BF16)

        _pipelined_heads(scores, softmax, weighted_values)

        yT = jnp.dot(woT_ref[...], aT[...].reshape(D_MODEL, ts), preferred_element_type=F32)
        y_ref[...] = x_ref[...] + yT.T.reshape(ns, tn, D_MODEL)


def _const_spec(shape):
    nd = len(shape)
    return pl.BlockSpec(shape, lambda i, _nd=nd: (0,) * _nd)


def _nbytes(shape, dtype):
    return math.prod(shape) * jnp.dtype(dtype).itemsize


def _vmem_limit(block_bytes, scratch_bytes, temp_bytes):
    est = 2 * block_bytes + scratch_bytes + temp_bytes
    return int(min(VMEM_BYTES - 4 * 2**20, max(est, 32 * 2**20)))


def _weight_operands(p):
    return [p["ng"], p["wk"], p["wqT"], p["wvT"], p["wgT"], p["woT"], p["gmat"], p["gq"], p["gk"]]


def _prompt_layer(x, p, cfg, tile):
    s = x.shape[0]
    hw, kvd = cfg.hist, cfg.kvd
    nq = tile // QUAD
    consts = _weight_operands(p) + [p["bias"]]
    operands = [x] + consts
    in_specs = [pl.BlockSpec((tile, D_MODEL), lambda t: (t, 0))] + [_const_spec(c.shape) for c in consts]
    if cfg.use_sinks:
        operands = [p["sinks"]] + operands
        in_specs = [pl.BlockSpec(memory_space=pltpu.SMEM)] + in_specs
    out_shape = (jax.ShapeDtypeStruct((s, D_MODEL), F32),
                 jax.ShapeDtypeStruct((hw, kvd), F32),
                 jax.ShapeDtypeStruct((hw, kvd), F32))
    out_specs = (pl.BlockSpec((tile, D_MODEL), lambda t: (t, 0)),
                 _const_spec((hw, kvd)), _const_spec((hw, kvd)))
    scratch = [
        ((kvd // LANES, hw + tile, LANES), BF16),
        ((cfg.n_kv, HEAD_DIM, hw + tile), BF16),
        ((N_HEADS, 2 * HEAD_DIM, tile), BF16),
        ((N_HEADS, HEAD_DIM, tile), F32),
        ((N_HEADS, HEAD_DIM, tile), BF16),
        ((2 * nq, cfg.pair_win, LANES), F32),
        ((2, nq, cfg.quad_win, QUAD), F32),
        ((2, nq, cfg.quad_win, QUAD), BF16),
        ((2, nq, 1, QUAD), F32),
    ]
    block_bytes = (2 * _nbytes((tile, D_MODEL), F32) + 2 * _nbytes((hw, kvd), F32)
                   + sum(_nbytes(c.shape, c.dtype) for c in consts))
    scratch_bytes = sum(_nbytes(sh, dt) for sh, dt in scratch)
    temp_bytes = 6 * _nbytes((D_MODEL, tile), F32) + 4 * _nbytes((cfg.quad_win, QUAD), F32)
    return pl.pallas_call(
        functools.partial(_prompt_layer_kernel, cfg=cfg, tile=tile),
        grid=(s // tile,),
        in_specs=in_specs,
        out_specs=out_specs,
        out_shape=out_shape,
        scratch_shapes=[pltpu.VMEM(sh, dt) for sh, dt in scratch],
        compiler_params=pltpu.CompilerParams(
            dimension_semantics=("arbitrary",),
            vmem_limit_bytes=_vmem_limit(block_bytes, scratch_bytes, temp_bytes)),
        name=f"prompt_layer_kv{cfg.n_kv}",
    )(*operands)


def _sample_layer(x, ck, cv, p, cfg):
    b, tn, _ = x.shape
    hw, kvd = cfg.hist, cfg.kvd
    ns = SAMPLE_GROUP
    ts, ws = ns * tn, hw + tn
    assert tn == CHUNK and b % ns == 0
    bias_s = p["bias"][:, :ws, :tn]
    bias_s = jnp.concatenate([bias_s] * (LANES // tn), axis=2)
    consts = _weight_operands(p) + [bias_s]
    operands = [x, ck, cv] + consts
    seq_spec = pl.BlockSpec((None, hw, kvd), lambda g, i: (g * ns + i, 0, 0))
    grp_spec = pl.BlockSpec((ns, tn, D_MODEL), lambda g, i: (g, 0, 0))
    in_specs = [grp_spec, seq_spec, seq_spec] + [
        pl.BlockSpec(c.shape, lambda g, i, _nd=c.ndim: (0,) * _nd) for c in consts]
    if cfg.use_sinks:
        operands = [p["sinks"]] + operands
        in_specs = [pl.BlockSpec(memory_space=pltpu.SMEM)] + in_specs
    out_shape = (jax.ShapeDtypeStruct((b, tn, D_MODEL), F32),
                 jax.ShapeDtypeStruct((b, hw, kvd), F32),
                 jax.ShapeDtypeStruct((b, hw, kvd), F32))
    out_specs = (grp_spec, seq_spec, seq_spec)
    scratch = [
        ((ns, kvd // LANES, ws, LANES), BF16),
        ((ns, cfg.n_kv, HEAD_DIM, ws), BF16),
        ((ns, tn, kvd), F32),
        ((ns, tn, kvd), F32),
        ((N_HEADS, 2 * HEAD_DIM, ts), BF16),
        ((N_HEADS, HEAD_DIM, ts), F32),
        ((N_HEADS, HEAD_DIM, ts), BF16),
        ((2, ws, ts), F32),
        ((2, ws, ts), BF16),
        ((2, 1, ts), F32),
    ]
    block_bytes = (2 * _nbytes((ts, D_MODEL), F32) + 4 * _nbytes((hw, kvd), F32)
                   + sum(_nbytes(c.shape, c.dtype) for c in consts))
    scratch_bytes = sum(_nbytes(sh, dt) for sh, dt in scratch)
    temp_bytes = 6 * _nbytes((D_MODEL, ts), F32) + 3 * _nbytes((hw + LANES, kvd), F32)
    return pl.pallas_call(
        functools.partial(_sample_layer_kernel, cfg=cfg),
        grid=(b // ns, ns),
        in_specs=in_specs,
        out_specs=out_specs,
        out_shape=out_shape,
        scratch_shapes=[pltpu.VMEM(sh, dt) for sh, dt in scratch],
        compiler_params=pltpu.CompilerParams(
            dimension_semantics=("arbitrary", "arbitrary"),
            vmem_limit_bytes=_vmem_limit(block_bytes, scratch_bytes, temp_bytes)),
        name=f"sample_layer_kv{cfg.n_kv}",
    )(*operands)


def _t5_bucket(rel):
    nb = T5_BUCKETS // 2
    max_exact = nb // 2
    ret = jnp.where(rel > 0, nb, 0)
    n = jnp.abs(rel)
    nf = jnp.maximum(n, 1).astype(F32)
    large = max_exact + (jnp.log(nf / max_exact) / math.log(T5_MAX_DIST / max_exact)
                         * (nb - max_exact)).astype(jnp.int32)
    large = jnp.minimum(large, nb - 1)
    return ret + jnp.where(n < max_exact, n, large)


def _pair_bias(cfg, bias_of_offset):
    hw, wp = cfg.hist, cfg.pair_win
    period = wp + PAIR
    idx = np.arange(period)
    offset = np.where(idx < PAIR, hw + idx, hw + idx - period)
    gen = bias_of_offset(jnp.asarray(offset, jnp.int32)).astype(F32)
    flat = jnp.tile(gen, (1, wp))[:, :wp * (period - 1)]
    b = flat.reshape(N_HEADS, wp, period - 1)[:, :, :PAIR]
    qc = (hw + np.arange(PAIR)) // CHUNK
    kc = np.arange(wp) // CHUNK
    valid = (kc[:, None] <= qc[None, :]) & (kc[:, None] >= qc[None, :] - cfg.past_chunks)
    return jnp.where(jnp.asarray(valid)[None], b, NEG_INF)


def _prep_layer(cfg, norm_g, w_in, q_g, k_g, w_out, bias, sinks):
    wq, wkv = N_HEADS * HEAD_DIM, cfg.kvd
    gn = cfg.gnorm
    blk = jnp.arange(gn) // HEAD_DIM
    p = {
        "ng": norm_g[None, :],
        "wk": w_in[:, wq:wq + wkv].astype(BF16),
        "wqT": w_in[:, :wq].T.astype(BF16),
        "wvT": w_in[:, wq + wkv:wq + 2 * wkv].T.astype(BF16),
        "wgT": w_in[:, wq + 2 * wkv:].T.astype(BF16),
        "woT": w_out.T.astype(BF16),
        "gmat": (blk[:, None] == blk[None, :]).astype(BF16),
        "gq": jnp.broadcast_to((q_g * (HEAD_DIM ** -0.5))[:, None], (HEAD_DIM, LANES)).astype(F32),
        "gk": jnp.tile(k_g, cfg.n_kv)[None, :],
        "bias": bias,
    }
    if cfg.use_sinks:
        p["sinks"] = sinks.astype(F32)
    return p


def _prompt_tile(cfg):
    return QUAD


def kernel(x_prompt, x_sample, cache_k_a, cache_v_a, cache_k_b, cache_v_b, t5_table, norm_a, w_in_a, q_norm_a, k_norm_a, sinks_a, w_out_a, norm_b, w_in_b, q_norm_b, k_norm_b, rel_bias_b, w_out_b):
    bsz, seq, _ = x_prompt.shape
    assert bsz == 1 and seq % QUAD == 0
    db = x_sample.shape[0]

    def t5_fn(offset):
        return t5_table[_t5_bucket(-offset)].T

    def rel_fn(offset):
        return rel_bias_b[:, jnp.clip(offset, -REL_CLIP, REL_CLIP) + REL_CLIP]

    pa = _prep_layer(CFG_A, norm_a, w_in_a, q_norm_a, k_norm_a, w_out_a,
                     _pair_bias(CFG_A, t5_fn), sinks_a)
    pb = _prep_layer(CFG_B, norm_b, w_in_b, q_norm_b, k_norm_b, w_out_b,
                     _pair_bias(CFG_B, rel_fn), None)

    xp = x_prompt.reshape(seq, D_MODEL)
    yp_a, k_a_p, v_a_p = _prompt_layer(xp, pa, CFG_A, _prompt_tile(CFG_A))
    yp_b, k_b_p, v_b_p = _prompt_layer(yp_a, pb, CFG_B, _prompt_tile(CFG_B))

    ys_a, k_a_s, v_a_s = _sample_layer(
        x_sample, cache_k_a.reshape(db, CFG_A.hist, CFG_A.kvd),
        cache_v_a.reshape(db, CFG_A.hist, CFG_A.kvd), pa, CFG_A)
    ys_b, k_b_s, v_b_s = _sample_layer(
        ys_a, cache_k_b.reshape(db, CFG_B.hist, CFG_B.kvd),
        cache_v_b.reshape(db, CFG_B.hist, CFG_B.kvd), pb, CFG_B)

    return (yp_b.reshape(1, seq, D_MODEL), ys_b,
            k_a_p.reshape(1, CFG_A.hist, CFG_A.n_kv, HEAD_DIM),
            v_a_p.reshape(1, CFG_A.hist, CFG_A.n_kv, HEAD_DIM),
            k_b_p.reshape(1, CFG_B.hist, CFG_B.n_kv, HEAD_DIM),
            v_b_p.reshape(1, CFG_B.hist, CFG_B.n_kv, HEAD_DIM),
            k_a_s.reshape(db, CFG_A.hist, CFG_A.n_kv, HEAD_DIM),
            v_a_s.reshape(db, CFG_A.hist, CFG_A.n_kv, HEAD_DIM),
            k_b_s.reshape(db, CFG_B.hist, CFG_B.n_kv, HEAD_DIM),
            v_b_s.reshape(db, CFG_B.hist, CFG_B.n_kv, HEAD_DIM))
```

```python
import functools
import math
from typing import NamedTuple

import jax
import jax.numpy as jnp
import numpy as np
from jax import lax
from jax.experimental import pallas as pl
from jax.experimental.pallas import tpu as pltpu

D_MODEL = 1024
CHUNK = 64
HEAD_DIM = 64
N_HEADS = 16
N_PAIRS = N_HEADS // 2
REL_CLIP = 128
T5_BUCKETS = 32
T5_MAX_DIST = 128
BIAS_SATURATION = max(REL_CLIP, T5_MAX_DIST)
EPS = 1e-6
NEG_INF = -1e30

LANES = 128
QUAD = 256
PAIR = LANES
SAMPLE_GROUP = QUAD // CHUNK
MXU_DIM = 256
VMEM_BYTES = 64 * 2**20

BF16 = jnp.bfloat16
F32 = jnp.float32
NT_DIMS = (((1,), (1,)), ((), ()))


class LayerCfg(NamedTuple):
    n_kv: int
    past_chunks: int
    use_sinks: bool

    @property
    def kvd(self):
        return self.n_kv * HEAD_DIM

    @property
    def group(self):
        return N_HEADS // self.n_kv

    @property
    def hist(self):
        return self.past_chunks * CHUNK

    @property
    def pair_win(self):
        return self.hist + PAIR

    @property
    def quad_win(self):
        return self.hist + QUAD

    @property
    def gnorm(self):
        return min(self.kvd, MXU_DIM)

    @property
    def v_per_pair(self):
        return self.group == 1

    @property
    def slice_rows(self):
        return (3 if self.v_per_pair else 2) * 2 * HEAD_DIM


CFG_A = LayerCfg(n_kv=2, past_chunks=2, use_sinks=True)
CFG_B = LayerCfg(n_kv=16, past_chunks=8, use_sinks=False)


def _rms_rows(x, g_row):
    ms = jnp.mean(x * x, axis=-1, keepdims=True)
    return x * lax.rsqrt(ms + EPS) * g_row


def _project_k(h, wk_ref, gmat_ref, gk_ref, cfg):
    k = jnp.dot(h, wk_ref[...], preferred_element_type=F32)
    ksq = (k * k).astype(BF16)
    gm = gmat_ref[...]
    gn = cfg.gnorm
    ss = jnp.concatenate(
        [jnp.dot(ksq[:, c:c + gn], gm, preferred_element_type=F32)
         for c in range(0, cfg.kvd, gn)], axis=1)
    return k * lax.rsqrt(ss * (1.0 / HEAD_DIM) + EPS) * gk_ref[...]


def _norm_q(q3, gq_ref):
    t = q3.shape[-1]
    ms = jnp.mean(q3 * q3, axis=1, keepdims=True)
    g = gq_ref[...][:, :t] if t < LANES else jnp.concatenate([gq_ref[...]] * (t // LANES), axis=1)
    return (q3 * lax.rsqrt(ms + EPS) * g[None]).astype(BF16)


def _silu(x):
    return x * jax.nn.sigmoid(x)


def _softmax_cols(s, sink):
    m = jnp.max(s, axis=0, keepdims=True)
    if sink is not None:
        m = jnp.maximum(m, sink)
    p = jnp.exp(s - m)
    l = jnp.sum(p, axis=0, keepdims=True)
    if sink is not None:
        l = l + jnp.exp(sink - m)
    return p, 1.0 / l


def _pipelined_heads(scores, softmax, weighted_values, prepare_pair=None):
    if prepare_pair is not None:
        prepare_pair(0)
        prepare_pair(1)
    scores(0, 0)
    scores(1, 1)
    softmax(0, 0)

    def head_pair(j):
        odd = 2 * j + 1
        scores(odd + 1, 0)
        weighted_values(odd - 1, 0)
        softmax(odd, 1)
        scores(odd + 2, 1)
        weighted_values(odd, 1)
        softmax(odd + 1, 0)

    def trip(j, carry):
        if prepare_pair is not None:
            prepare_pair(j + 2)
        head_pair(j)
        return carry

    if prepare_pair is None:
        lax.fori_loop(0, N_PAIRS - 1, trip, 0)
    else:
        for j in range(N_PAIRS - 2):
            trip(j, 0)
        head_pair(N_PAIRS - 2)
    weighted_values(N_HEADS - 2, 0)
    softmax(N_HEADS - 1, 1)
    weighted_values(N_HEADS - 1, 1)


def _prompt_layer_kernel(*refs, cfg, tile):
    refs = list(refs)
    sinks_ref = refs.pop(0) if cfg.use_sinks else None
    x_ref, ng_ref, wk_ref = refs[:3]
    refs = refs[3:]
    wvT_ref = None if cfg.v_per_pair else refs.pop(0)
    (wsl_ref, woT_ref, gmat_ref, gq_ref, gk_ref, bias_ref, y_ref, kkeep_ref, vkeep_ref,
     hbuf, khist, vhist, vlast, qpad, gts, aT, maskbuf, sbuf, pbuf, lbuf) = refs
    t = pl.program_id(0)
    nt = pl.num_programs(0)
    hw, wp, w = cfg.hist, cfg.pair_win, cfg.quad_win
    nq = tile // QUAD
    pair_rows = 2 * HEAD_DIM

    @pl.when(t == 0)
    def _zero_history():
        khist[:, :hw, :] = jnp.zeros((khist.shape[0], hw, LANES), BF16)
        vhist[:, :, :hw] = jnp.zeros((cfg.n_kv, HEAD_DIM, hw), BF16)

    @pl.when(t > 0)
    def _roll_history():
        khist[:, 0:hw, :] = khist[:, tile:tile + hw, :]
        vhist[:, :, 0:hw] = vhist[:, :, tile:tile + hw]

    @pl.when(t == 0)
    def _zero_prob_corners():
        zpad = jnp.zeros((2, nq, PAIR, LANES), BF16)
        pbuf[:, :, wp:w, 0:PAIR] = zpad
        pbuf[:, :, 0:PAIR, PAIR:QUAD] = zpad

    n_invalid = jnp.maximum(hw - t * tile, 0)
    rows = lax.broadcasted_iota(jnp.int32, (wp, LANES), 0)
    for qi in range(nq):
        for lt in range(2):
            off = qi * QUAD + lt * PAIR
            maskbuf[2 * qi + lt] = jnp.where(rows + off < n_invalid, NEG_INF, 0.0).astype(F32)

    x = x_ref[...]
    h = _rms_rows(x, ng_ref[...]).astype(BF16)
    hbuf[...] = h

    kn = _project_k(h, wk_ref, gmat_ref, gk_ref, cfg)
    for kb in range(cfg.kvd // LANES):
        khist[kb, hw:hw + tile, :] = kn[:, kb * LANES:(kb + 1) * LANES].astype(BF16)

    if not cfg.v_per_pair:
        vT = lax.dot_general(wvT_ref[...], h, NT_DIMS, preferred_element_type=F32)
        vhist[:, :, hw:hw + tile] = vT.reshape(cfg.n_kv, HEAD_DIM, tile).astype(BF16)
        vlast[...] = vT

    if tile >= hw:
        @pl.when(t == nt - 1)
        def _keep_k():
            kkeep_ref[...] = kn[tile - hw:, :]
    else:
        @pl.when(t >= nt - hw // tile)
        def _keep_k():
            row0 = pl.multiple_of(hw - (nt - t) * tile, tile)
            kkeep_ref[pl.ds(row0, tile), :] = kn

    zq = jnp.zeros((HEAD_DIM, tile), BF16)

    def prepare_pair(j):
        pT = lax.dot_general(wsl_ref[j], hbuf[...], NT_DIMS, preferred_element_type=F32)
        qn = _norm_q(pT[0:pair_rows].reshape(2, HEAD_DIM, tile), gq_ref)
        for i in range(2):
            if cfg.v_per_pair:
                halves = [zq, qn[i]] if i else [qn[i], zq]
            else:
                high = ((2 * j + i) // cfg.group) % 2 == 1
                halves = [jnp.where(high, zq, qn[i]), jnp.where(high, qn[i], zq)]
            qpad[2 * j + i] = jnp.concatenate(halves, axis=0)
        gts[pl.ds(2 * j, 2)] = _silu(pT[pair_rows:2 * pair_rows]).reshape(2, HEAD_DIM, tile)
        if cfg.v_per_pair:
            vT2 = pT[2 * pair_rows:3 * pair_rows]
            vhist[pl.ds(2 * j, 2), :, hw:hw + tile] = vT2.reshape(2, HEAD_DIM, tile).astype(BF16)
            vrow = j * pair_rows if isinstance(j, int) else pl.multiple_of(j * pair_rows, pair_rows)
            vlast[pl.ds(vrow, pair_rows), :] = vT2

    def scores(hd, slot):
        kb = (hd // cfg.group) // 2
        for qi in range(nq):
            r0 = qi * QUAD
            sbuf[slot, qi] = jnp.dot(khist[kb, r0:r0 + w, :], qpad[hd, :, r0:r0 + QUAD],
                                     preferred_element_type=F32)

    def softmax(hd, slot):
        sink = sinks_ref[hd] if cfg.use_sinks else None
        for qi in range(nq):
            for lt in range(2):
                lanes = slice(lt * PAIR, (lt + 1) * PAIR)
                keys = slice(lt * PAIR, lt * PAIR + wp)
                s = sbuf[slot, qi, keys, lanes] + bias_ref[hd] + maskbuf[2 * qi + lt]
                p, li = _softmax_cols(s, sink)
                pbuf[slot, qi, keys, lanes] = p.astype(BF16)
                lbuf[slot, qi, :, lanes] = li

    def weighted_values(hd, slot):
        kvh = hd // cfg.group
        for qi in range(nq):
            r0 = qi * QUAD
            oT = jnp.dot(vhist[kvh, :, r0:r0 + w], pbuf[slot, qi], preferred_element_type=F32)
            o = oT * lbuf[slot, qi]
            aT[hd, :, r0:r0 + QUAD] = (o * gts[hd, :, r0:r0 + QUAD]).astype(BF16)

    _pipelined_heads(scores, softmax, weighted_values, prepare_pair)

    yT = jnp.dot(woT_ref[...], aT[...].reshape(D_MODEL, tile), preferred_element_type=F32)
    y_ref[...] = x + yT.T

    if tile >= hw:
        @pl.when(t == nt - 1)
        def _keep_v():
            vkeep_ref[...] = vlast[:, tile - hw:].T
    else:
        @pl.when(t >= nt - hw // tile)
        def _keep_v():
            row0 = pl.multiple_of(hw - (nt - t) * tile, tile)
            vkeep_ref[pl.ds(row0, tile), :] = vlast[...].T


def _sample_layer_kernel(*refs, cfg):
    refs = list(refs)
    sinks_ref = refs.pop(0) if cfg.use_sinks else None
    x_ref, ck_ref, cv_ref, ng_ref, wk_ref = refs[:5]
    refs = refs[5:]
    wvT_ref = None if cfg.v_per_pair else refs.pop(0)
    (wsl_ref, woT_ref, gmat_ref, gq_ref, gk_ref, bias_ref, y_ref, newk_ref, newv_ref,
     kwin, vwin, knew, vnew, qpad, gts, aT, sbuf, pbuf, lbuf) = refs
    b = pl.program_id(1)
    hw, kvd = cfg.hist, cfg.kvd
    tn, ns = CHUNK, SAMPLE_GROUP
    ts, ws = ns * tn, hw + tn
    pair_rows = 2 * HEAD_DIM

    @pl.when(b == 0)
    def _project():
        x = x_ref[...].reshape(ts, D_MODEL)
        h = _rms_rows(x, ng_ref[...]).astype(BF16)
        knew[...] = _project_k(h, wk_ref, gmat_ref, gk_ref, cfg).reshape(ns, tn, kvd)
        wsl = wsl_ref[...].reshape(N_PAIRS * cfg.slice_rows, D_MODEL)
        pT = lax.dot_general(wsl, h, NT_DIMS, preferred_element_type=F32)
        pT = pT.reshape(N_PAIRS, cfg.slice_rows, ts)
        qn = _norm_q(pT[:, 0:pair_rows].reshape(N_HEADS, HEAD_DIM, ts), gq_ref)
        zq = jnp.zeros((HEAD_DIM, ts), BF16)
        for hd in range(N_HEADS):
            high = (hd // cfg.group) % 2 == 1
            qpad[hd] = jnp.concatenate([zq, qn[hd]] if high else [qn[hd], zq], axis=0)
        gts[...] = _silu(pT[:, pair_rows:2 * pair_rows]).reshape(N_HEADS, HEAD_DIM, ts)
        if cfg.v_per_pair:
            vT = pT[:, 2 * pair_rows:3 * pair_rows].reshape(kvd, ts)
        else:
            vT = lax.dot_general(wvT_ref[...], h, NT_DIMS, preferred_element_type=F32)
        vnew[...] = vT.T.reshape(ns, tn, kvd)

    ck = ck_ref[...]
    cv = cv_ref[...]
    kn = knew[b]
    vn = vnew[b]
    newk_ref[0:hw - tn, :] = ck[tn:, :]
    newk_ref[hw - tn:, :] = kn
    newv_ref[0:hw - tn, :] = cv[tn:, :]
    newv_ref[hw - tn:, :] = vn
    for kb in range(kvd // LANES):
        kwin[b, kb, 0:hw, :] = ck[:, kb * LANES:(kb + 1) * LANES].astype(BF16)
        kwin[b, kb, hw:ws, :] = kn[:, kb * LANES:(kb + 1) * LANES].astype(BF16)
    vrows = jnp.concatenate([cv, vn, jnp.zeros((LANES - tn, kvd), F32)], axis=0)
    vwin[b] = vrows.T[:, :ws].reshape(cfg.n_kv, HEAD_DIM, ws).astype(BF16)

    @pl.when(b == ns - 1)
    def _attend():
        wlane = lax.broadcasted_iota(jnp.int32, (pair_rows, ts), 1) // tn
        wmask = [(wlane == i).astype(F32).astype(BF16) for i in range(ns)]
        olane = lax.broadcasted_iota(jnp.int32, (HEAD_DIM, ts), 1) // tn

        def scores(hd, slot):
            kb = (hd // cfg.group) // 2
            q = qpad[hd]
            wbd = jnp.concatenate([q * wmask[i] for i in range(ns)], axis=0)
            kcat = jnp.concatenate([kwin[i, kb] for i in range(ns)], axis=1)
            sbuf[slot] = jnp.dot(kcat, wbd, preferred_element_type=F32)

        def softmax(hd, slot):
            sink = sinks_ref[hd] if cfg.use_sinks else None
            for lt in range(ts // LANES):
                lanes = slice(lt * LANES, (lt + 1) * LANES)
                p, li = _softmax_cols(sbuf[slot, :, lanes] + bias_ref[hd], sink)
                pbuf[slot, :, lanes] = p.astype(BF16)
                lbuf[slot, :, lanes] = li

        def weighted_values(hd, slot):
            kvh = hd // cfg.group
            vcat = jnp.concatenate([vwin[i, kvh] for i in range(ns)], axis=0)
            r = jnp.dot(vcat, pbuf[slot], preferred_element_type=F32)
            o = r[0:HEAD_DIM]
            for i in range(1, ns):
                o = jnp.where(olane == i, r[i * HEAD_DIM:(i + 1) * HEAD_DIM], o)
            aT[hd] = (o * lbuf[slot] * gts[hd]).astype(BF16)

        _pipelined_heads(scores, softmax, weighted_values)

        yT = jnp.dot(woT_ref[...], aT[...].reshape(D_MODEL, ts), preferred_element_type=F32)
        y_ref[...] = x_ref[...] + yT.T.reshape(ns, tn, D_MODEL)


def _nbytes(shape, dtype):
    return math.prod(shape) * jnp.dtype(dtype).itemsize


def _vmem_limit(block_bytes, scratch_bytes, temp_bytes):
    est = 2 * block_bytes + scratch_bytes + temp_bytes
    return int(min(VMEM_BYTES - 4 * 2**20, max(est, 32 * 2**20)))


def _weight_operands(p, cfg):
    ops = [p["ng"], p["wk"]]
    if not cfg.v_per_pair:
        ops.append(p["wvT"])
    return ops + [p["wsl"], p["woT"], p["gmat"], p["gq"], p["gk"]]


def _prompt_layer(x, p, cfg, tile):
    s = x.shape[0]
    hw, kvd = cfg.hist, cfg.kvd
    nq = tile // QUAD
    consts = _weight_operands(p, cfg) + [p["bias"]]
    operands = [x] + consts
    in_specs = [pl.BlockSpec((tile, D_MODEL), lambda t: (t, 0))] + [
        pl.BlockSpec(c.shape, lambda t, _nd=c.ndim: (0,) * _nd) for c in consts]
    if cfg.use_sinks:
        operands = [p["sinks"]] + operands
        in_specs = [pl.BlockSpec(memory_space=pltpu.SMEM)] + in_specs
    out_shape = (jax.ShapeDtypeStruct((s, D_MODEL), F32),
                 jax.ShapeDtypeStruct((hw, kvd), F32),
                 jax.ShapeDtypeStruct((hw, kvd), F32))
    keep_spec = pl.BlockSpec((hw, kvd), lambda t: (0, 0))
    out_specs = (pl.BlockSpec((tile, D_MODEL), lambda t: (t, 0)), keep_spec, keep_spec)
    scratch = [
        ((tile, D_MODEL), BF16),
        ((kvd // LANES, hw + tile, LANES), BF16),
        ((cfg.n_kv, HEAD_DIM, hw + tile), BF16),
        ((kvd, tile), F32),
        ((N_HEADS, 2 * HEAD_DIM, tile), BF16),
        ((N_HEADS, HEAD_DIM, tile), F32),
        ((N_HEADS, HEAD_DIM, tile), BF16),
        ((2 * nq, cfg.pair_win, LANES), F32),
        ((2, nq, cfg.quad_win, QUAD), F32),
        ((2, nq, cfg.quad_win, QUAD), BF16),
        ((2, nq, 1, QUAD), F32),
    ]
    block_bytes = (2 * _nbytes((tile, D_MODEL), F32) + 2 * _nbytes((hw, kvd), F32)
                   + sum(_nbytes(c.shape, c.dtype) for c in consts))
    scratch_bytes = sum(_nbytes(sh, dt) for sh, dt in scratch)
    temp_bytes = 6 * _nbytes((D_MODEL, tile), F32) + 4 * _nbytes((cfg.quad_win, QUAD), F32)
    return pl.pallas_call(
        functools.partial(_prompt_layer_kernel, cfg=cfg, tile=tile),
        grid=(s // tile,),
        in_specs=in_specs,
        out_specs=out_specs,
        out_shape=out_shape,
        scratch_shapes=[pltpu.VMEM(sh, dt) for sh, dt in scratch],
        compiler_params=pltpu.CompilerParams(
            dimension_semantics=("arbitrary",),
            vmem_limit_bytes=_vmem_limit(block_bytes, scratch_bytes, temp_bytes)),
        name=f"prompt_layer_kv{cfg.n_kv}",
    )(*operands)


def _sample_layer(x, ck, cv, p, cfg):
    b, tn, _ = x.shape
    hw, kvd = cfg.hist, cfg.kvd
    ns = SAMPLE_GROUP
    ts, ws = ns * tn, hw + tn
    assert tn == CHUNK and b % ns == 0
    bias_s = p["bias"][:, :ws, :tn]
    bias_s = jnp.concatenate([bias_s] * (LANES // tn), axis=2)
    consts = _weight_operands(p, cfg) + [bias_s]
    operands = [x, ck, cv] + consts
    seq_spec = pl.BlockSpec((None, hw, kvd), lambda g, i: (g * ns + i, 0, 0))
    grp_spec = pl.BlockSpec((ns, tn, D_MODEL), lambda g, i: (g, 0, 0))
    in_specs = [grp_spec, seq_spec, seq_spec] + [
        pl.BlockSpec(c.shape, lambda g, i, _nd=c.ndim: (0,) * _nd) for c in consts]
    if cfg.use_sinks:
        operands = [p["sinks"]] + operands
        in_specs = [pl.BlockSpec(memory_space=pltpu.SMEM)] + in_specs
    out_shape = (jax.ShapeDtypeStruct((b, tn, D_MODEL), F32),
                 jax.ShapeDtypeStruct((b, hw, kvd), F32),
                 jax.ShapeDtypeStruct((b, hw, kvd), F32))
    out_specs = (grp_spec, seq_spec, seq_spec)
    scratch = [
        ((ns, kvd // LANES, ws, LANES), BF16),
        ((ns, cfg.n_kv, HEAD_DIM, ws), BF16),
        ((ns, tn, kvd), F32),
        ((ns, tn, kvd), F32),
        ((N_HEADS, 2 * HEAD_DIM, ts), BF16),
        ((N_HEADS, HEAD_DIM, ts), F32),
        ((N_HEADS, HEAD_DIM, ts), BF16),
        ((2, ws, ts), F32),
        ((2, ws, ts), BF16),
        ((2, 1, ts), F32),
    ]
    block_bytes = (2 * _nbytes((ts, D_MODEL), F32) + 4 * _nbytes((hw, kvd), F32)
                   + sum(_nbytes(c.shape, c.dtype) for c in consts))
    scratch_bytes = sum(_nbytes(sh, dt) for sh, dt in scratch)
    temp_bytes = 6 * _nbytes((D_MODEL, ts), F32) + 3 * _nbytes((hw + LANES, kvd), F32)
    return pl.pallas_call(
        functools.partial(_sample_layer_kernel, cfg=cfg),
        grid=(b // ns, ns),
        in_specs=in_specs,
        out_specs=out_specs,
        out_shape=out_shape,
        scratch_shapes=[pltpu.VMEM(sh, dt) for sh, dt in scratch],
        compiler_params=pltpu.CompilerParams(
            dimension_semantics=("arbitrary", "arbitrary"),
            vmem_limit_bytes=_vmem_limit(block_bytes, scratch_bytes, temp_bytes)),
        name=f"sample_layer_kv{cfg.n_kv}",
    )(*operands)


def _t5_bucket(rel):
    nb = T5_BUCKETS // 2
    max_exact = nb // 2
    ret = jnp.where(rel > 0, nb, 0)
    n = jnp.abs(rel)
    nf = jnp.maximum(n, 1).astype(F32)
    large = max_exact + (jnp.log(nf / max_exact) / math.log(T5_MAX_DIST / max_exact)
                         * (nb - max_exact)).astype(jnp.int32)
    large = jnp.minimum(large, nb - 1)
    return ret + jnp.where(n < max_exact, n, large)


def _pair_bias(cfg, bias_of_offset):
    hw, wp = cfg.hist, cfg.pair_win
    row0 = max(hw - BIAS_SATURATION, 0)
    band = wp - row0
    period = band + PAIR
    idx = np.arange(period)
    offset = hw - row0 + np.where(idx < PAIR, idx, idx - period)
    gen = bias_of_offset(jnp.asarray(offset, jnp.int32)).astype(F32)
    flat = jnp.tile(gen, (1, band))[:, :band * (period - 1)]
    b = flat.reshape(N_HEADS, band, period - 1)[:, :, :PAIR]
    far = bias_of_offset(jnp.full((1,), BIAS_SATURATION, jnp.int32)).astype(F32)
    b = jnp.concatenate([jnp.broadcast_to(far[:, :, None], (N_HEADS, row0, PAIR)), b], axis=1)
    qc = (hw + np.arange(PAIR)) // CHUNK
    kc = np.arange(wp) // CHUNK
    valid = (kc[:, None] <= qc[None, :]) & (kc[:, None] >= qc[None, :] - cfg.past_chunks)
    return jnp.where(jnp.asarray(valid)[None], b, NEG_INF)


def _prep_layer(cfg, norm_g, w_in, q_g, k_g, w_out, bias, sinks):
    wq, wkv = N_HEADS * HEAD_DIM, cfg.kvd
    pair_rows = 2 * HEAD_DIM
    gn = cfg.gnorm
    blk = jnp.arange(gn) // HEAD_DIM
    w_inT = w_in.T.astype(BF16)
    parts = [w_inT[:wq].reshape(N_PAIRS, pair_rows, D_MODEL),
             w_inT[wq + 2 * wkv:].reshape(N_PAIRS, pair_rows, D_MODEL)]
    if cfg.v_per_pair:
        parts.append(w_inT[wq + wkv:wq + 2 * wkv].reshape(N_PAIRS, pair_rows, D_MODEL))
    p = {
        "ng": norm_g[None, :],
        "wk": w_in[:, wq:wq + wkv].astype(BF16),
        "wsl": jnp.concatenate(parts, axis=1),
        "woT": w_out.T.astype(BF16),
        "gmat": (blk[:, None] == blk[None, :]).astype(BF16),
        "gq": jnp.broadcast_to((q_g * (HEAD_DIM ** -0.5))[:, None], (HEAD_DIM, LANES)).astype(F32),
        "gk": jnp.tile(k_g, cfg.n_kv)[None, :],
        "bias": bias,
    }
    if not cfg.v_per_pair:
        p["wvT"] = w_inT[wq + wkv:wq + 2 * wkv]
    if cfg.use_sinks:
        p["sinks"] = sinks.astype(F32)
    return p


def _prompt_tile(cfg):
    return QUAD


def kernel(x_prompt, x_sample, cache_k_a, cache_v_a, cache_k_b, cache_v_b, t5_table, norm_a, w_in_a, q_norm_a, k_norm_a, sinks_a, w_out_a, norm_b, w_in_b, q_norm_b, k_norm_b, rel_bias_b, w_out_b):
    bsz, seq, _ = x_prompt.shape
    assert bsz == 1 and seq % QUAD == 0
    db = x_sample.shape[0]

    def t5_fn(offset):
        return t5_table[_t5_bucket(-offset)].T

    def rel_fn(offset):
        return rel_bias_b[:, jnp.clip(offset, -REL_CLIP, REL_CLIP) + REL_CLIP]

    pa = _prep_layer(CFG_A, norm_a, w_in_a, q_norm_a, k_norm_a, w_out_a,
                     _pair_bias(CFG_A, t5_fn), sinks_a)
    pb = _prep_layer(CFG_B, norm_b, w_in_b, q_norm_b, k_norm_b, w_out_b,
                     _pair_bias(CFG_B, rel_fn), None)

    xp = x_prompt.reshape(seq, D_MODEL)
    yp_a, k_a_p, v_a_p = _prompt_layer(xp, pa, CFG_A, _prompt_tile(CFG_A))
    yp_b, k_b_p, v_b_p = _prompt_layer(yp_a, pb, CFG_B, _prompt_tile(CFG_B))

    ys_a, k_a_s, v_a_s = _sample_layer(
        x_sample, cache_k_a.reshape(db, CFG_A.hist, CFG_A.kvd),
        cache_v_a.reshape(db, CFG_A.hist, CFG_A.kvd), pa, CFG_A)
    ys_b, k_b_s, v_b_s = _sample_layer(
        ys_a, cache_k_b.reshape(db, CFG_B.hist, CFG_B.kvd),
        cache_v_b.reshape(db, CFG_B.hist, CFG_B.kvd), pb, CFG_B)

    return (yp_b.reshape(1, seq, D_MODEL), ys_b,
            k_a_p.reshape(1, CFG_A.hist, CFG_A.n_kv, HEAD_DIM),
            v_a_p.reshape(1, CFG_A.hist, CFG_A.n_kv, HEAD_DIM),
            k_b_p.reshape(1, CFG_B.hist, CFG_B.n_kv, HEAD_DIM),
            v_b_p.reshape(1, CFG_B.hist, CFG_B.n_kv, HEAD_DIM),
            k_a_s.reshape(db, CFG_A.hist, CFG_A.n_kv, HEAD_DIM),
            v_a_s.reshape(db, CFG_A.hist, CFG_A.n_kv, HEAD_DIM),
            k_b_s.reshape(db, CFG_B.hist, CFG_B.n_kv, HEAD_DIM),
            v_b_s.reshape(db, CFG_B.hist, CFG_B.n_kv, HEAD_DIM))
```

```python
import functools
import math
from typing import NamedTuple

import jax
import jax.numpy as jnp
import numpy as np
from jax import lax
from jax.experimental import pallas as pl
from jax.experimental.pallas import tpu as pltpu

D_MODEL = 1024
CHUNK = 64
HEAD_DIM = 64
N_HEADS = 16
N_PAIRS = N_HEADS // 2
REL_CLIP = 128
T5_BUCKETS = 32
T5_MAX_DIST = 128
BIAS_SATURATION = max(REL_CLIP, T5_MAX_DIST)
EPS = 1e-6
NEG_INF = -1e30
LOG2E = math.log2(math.e)

LANES = 128
QUAD = 256
PAIR = LANES
SAMPLE_GROUP = QUAD // CHUNK
MXU_DIM = 256
VMEM_BYTES = 64 * 2**20

BF16 = jnp.bfloat16
F32 = jnp.float32
NT_DIMS = (((1,), (1,)), ((), ()))


class LayerCfg(NamedTuple):
    n_kv: int
    past_chunks: int
    use_sinks: bool

    @property
    def kvd(self):
        return self.n_kv * HEAD_DIM

    @property
    def group(self):
        return N_HEADS // self.n_kv

    @property
    def hist(self):
        return self.past_chunks * CHUNK

    @property
    def pair_win(self):
        return self.hist + PAIR

    @property
    def quad_win(self):
        return self.hist + QUAD

    @property
    def gnorm(self):
        return min(self.kvd, MXU_DIM)

    @property
    def kv_per_pair(self):
        return self.group == 1

    @property
    def slice_rows(self):
        return (3 if self.kv_per_pair else 2) * 2 * HEAD_DIM


CFG_A = LayerCfg(n_kv=2, past_chunks=2, use_sinks=True)
CFG_B = LayerCfg(n_kv=16, past_chunks=8, use_sinks=False)


def _rms_rows(x, g_row):
    ms = jnp.mean(x * x, axis=-1, keepdims=True)
    return x * lax.rsqrt(ms + EPS) * g_row


def _project_k(h, wk_ref, gmat_ref, gk_ref, cfg):
    k = jnp.dot(h, wk_ref[...], preferred_element_type=F32)
    ksq = (k * k).astype(BF16)
    gm = gmat_ref[...]
    gn = cfg.gnorm
    ss = jnp.concatenate(
        [jnp.dot(ksq[:, c:c + gn], gm, preferred_element_type=F32)
         for c in range(0, cfg.kvd, gn)], axis=1)
    return k * lax.rsqrt(ss * (1.0 / HEAD_DIM) + EPS) * gk_ref[...]


def _norm_q(q3, gq_ref):
    t = q3.shape[-1]
    ms = jnp.mean(q3 * q3, axis=1, keepdims=True)
    g = gq_ref[...][:, :t] if t < LANES else jnp.concatenate([gq_ref[...]] * (t // LANES), axis=1)
    return (q3 * lax.rsqrt(ms + EPS) * g[None]).astype(BF16)


def _silu(x):
    return x * jax.nn.sigmoid(x)


def _softmax_cols(s, sink):
    m = jnp.max(s, axis=0, keepdims=True)
    if sink is not None:
        m = jnp.maximum(m, sink)
    p = jnp.exp2(s - m)
    l = jnp.sum(p, axis=0, keepdims=True)
    if sink is not None:
        l = l + jnp.exp2(sink - m)
    return p, 1.0 / l


def _pipelined_heads(scores, softmax, weighted_values, prepare_pair=None):
    if prepare_pair is not None:
        prepare_pair(0)
        prepare_pair(1)
    scores(0, 0)
    scores(1, 1)
    softmax(0, 0)

    def head_pair(j):
        odd = 2 * j + 1
        scores(odd + 1, 0)
        weighted_values(odd - 1, 0)
        softmax(odd, 1)
        scores(odd + 2, 1)
        weighted_values(odd, 1)
        softmax(odd + 1, 0)

    def trip(j, carry):
        if prepare_pair is not None:
            prepare_pair(j + 2)
        head_pair(j)
        return carry

    if prepare_pair is None:
        lax.fori_loop(0, N_PAIRS - 1, trip, 0)
    else:
        for j in range(N_PAIRS - 2):
            trip(j, 0)
        head_pair(N_PAIRS - 2)
    weighted_values(N_HEADS - 2, 0)
    softmax(N_HEADS - 1, 1)
    weighted_values(N_HEADS - 1, 1)


def _prompt_layer_kernel(*refs, cfg, tile):
    refs = list(refs)
    sinks_ref = refs.pop(0) if cfg.use_sinks else None
    x_ref, ng_ref, wk_ref = refs[:3]
    refs = refs[3:]
    wvT_ref = None if cfg.kv_per_pair else refs.pop(0)
    (wsl_ref, woT_ref, gmat_ref, gq_ref, gk_ref, bias_ref, y_ref, kkeep_ref, vkeep_ref,
     hbuf, khist, vhist, klast, vlast, qpad, gts, aT, maskbuf, sbuf, pbuf, lbuf) = refs
    t = pl.program_id(0)
    nt = pl.num_programs(0)
    hw, wp, w = cfg.hist, cfg.pair_win, cfg.quad_win
    nq = tile // QUAD
    pair_rows = 2 * HEAD_DIM

    @pl.when(t == 0)
    def _zero_history():
        khist[:, :hw, :] = jnp.zeros((khist.shape[0], hw, LANES), BF16)
        vhist[:, :, :hw] = jnp.zeros((cfg.n_kv, HEAD_DIM, hw), BF16)

    @pl.when(t > 0)
    def _roll_history():
        khist[:, 0:hw, :] = khist[:, tile:tile + hw, :]
        vhist[:, :, 0:hw] = vhist[:, :, tile:tile + hw]

    @pl.when(t == 0)
    def _zero_prob_corners():
        zpad = jnp.zeros((2, nq, PAIR, LANES), BF16)
        pbuf[:, :, wp:w, 0:PAIR] = zpad
        pbuf[:, :, 0:PAIR, PAIR:QUAD] = zpad

    n_invalid = jnp.maximum(hw - t * tile, 0)
    rows = lax.broadcasted_iota(jnp.int32, (wp, LANES), 0)
    for qi in range(nq):
        for lt in range(2):
            off = qi * QUAD + lt * PAIR
            maskbuf[2 * qi + lt] = jnp.where(rows + off < n_invalid, NEG_INF, 0.0).astype(F32)

    x = x_ref[...]
    h = _rms_rows(x, ng_ref[...]).astype(BF16)
    hbuf[...] = h

    kn = _project_k(h, wk_ref, gmat_ref, gk_ref, cfg)
    klast[...] = kn
    for kb in range(cfg.kvd // LANES):
        khist[kb, hw:hw + tile, :] = kn[:, kb * LANES:(kb + 1) * LANES].astype(BF16)

    if not cfg.kv_per_pair:
        vT = lax.dot_general(wvT_ref[...], h, NT_DIMS, preferred_element_type=F32)
        vhist[:, :, hw:hw + tile] = vT.reshape(cfg.n_kv, HEAD_DIM, tile).astype(BF16)
        vlast[...] = vT

    zq = jnp.zeros((HEAD_DIM, tile), BF16)

    def prepare_pair(j):
        pT = lax.dot_general(wsl_ref[j], hbuf[...], NT_DIMS, preferred_element_type=F32)
        qn = _norm_q(pT[0:pair_rows].reshape(2, HEAD_DIM, tile), gq_ref)
        for i in range(2):
            high = ((2 * j + i) // cfg.group) % 2 == 1
            qpad[2 * j + i] = jnp.concatenate([zq, qn[i]] if high else [qn[i], zq], axis=0)
        gts[2 * j:2 * j + 2] = _silu(pT[pair_rows:2 * pair_rows]).reshape(2, HEAD_DIM, tile)
        if cfg.kv_per_pair:
            vT2 = pT[2 * pair_rows:3 * pair_rows]
            vhist[2 * j:2 * j + 2, :, hw:hw + tile] = vT2.reshape(2, HEAD_DIM, tile).astype(BF16)
            vlast[j * pair_rows:(j + 1) * pair_rows, :] = vT2

    def scores(hd, slot):
        kb = (hd // cfg.group) // 2
        for qi in range(nq):
            r0 = qi * QUAD
            sbuf[slot, qi] = jnp.dot(khist[kb, r0:r0 + w, :], qpad[hd, :, r0:r0 + QUAD],
                                     preferred_element_type=F32)

    def softmax(hd, slot):
        sink = sinks_ref[hd] if cfg.use_sinks else None
        for qi in range(nq):
            for lt in range(2):
                lanes = slice(lt * PAIR, (lt + 1) * PAIR)
                keys = slice(lt * PAIR, lt * PAIR + wp)
                s = sbuf[slot, qi, keys, lanes] + bias_ref[hd] + maskbuf[2 * qi + lt]
                p, li = _softmax_cols(s, sink)
                pbuf[slot, qi, keys, lanes] = p.astype(BF16)
                lbuf[slot, qi, :, lanes] = li

    def weighted_values(hd, slot):
        kvh = hd // cfg.group
        for qi in range(nq):
            r0 = qi * QUAD
            oT = jnp.dot(vhist[kvh, :, r0:r0 + w], pbuf[slot, qi], preferred_element_type=F32)
            o = oT * lbuf[slot, qi]
            aT[hd, :, r0:r0 + QUAD] = (o * gts[hd, :, r0:r0 + QUAD]).astype(BF16)

    _pipelined_heads(scores, softmax, weighted_values, prepare_pair)

    yT = jnp.dot(woT_ref[...], aT[...].reshape(D_MODEL, tile), preferred_element_type=F32)
    y_ref[...] = x + yT.T

    if tile >= hw:
        @pl.when(t == nt - 1)
        def _keep():
            kkeep_ref[...] = klast[tile - hw:, :]
            vkeep_ref[...] = vlast[:, tile - hw:].T
    else:
        @pl.when(t >= nt - hw // tile)
        def _keep():
            row0 = pl.multiple_of(hw - (nt - t) * tile, tile)
            kkeep_ref[pl.ds(row0, tile), :] = klast[...]
            vkeep_ref[pl.ds(row0, tile), :] = vlast[...].T


def _sample_layer_kernel(*refs, cfg):
    refs = list(refs)
    sinks_ref = refs.pop(0) if cfg.use_sinks else None
    x_ref, ck_ref, cv_ref, ng_ref, wk_ref = refs[:5]
    refs = refs[5:]
    wvT_ref = None if cfg.kv_per_pair else refs.pop(0)
    (wsl_ref, woT_ref, gmat_ref, gq_ref, gk_ref, bias_ref, y_ref, newk_ref, newv_ref,
     kwin, vwin, knew, vnew, qpad, gts, aT, sbuf, pbuf, lbuf) = refs
    b = pl.program_id(1)
    hw, kvd = cfg.hist, cfg.kvd
    tn, ns = CHUNK, SAMPLE_GROUP
    ts, ws = ns * tn, hw + tn
    pair_rows = 2 * HEAD_DIM

    @pl.when(b == 0)
    def _project():
        x = x_ref[...].reshape(ts, D_MODEL)
        h = _rms_rows(x, ng_ref[...]).astype(BF16)
        knew[...] = _project_k(h, wk_ref, gmat_ref, gk_ref, cfg).reshape(ns, tn, kvd)
        wsl = wsl_ref[...].reshape(N_PAIRS * cfg.slice_rows, D_MODEL)
        pT = lax.dot_general(wsl, h, NT_DIMS, preferred_element_type=F32)
        pT = pT.reshape(N_PAIRS, cfg.slice_rows, ts)
        qn = _norm_q(pT[:, 0:pair_rows].reshape(N_HEADS, HEAD_DIM, ts), gq_ref)
        zq = jnp.zeros((HEAD_DIM, ts), BF16)
        for hd in range(N_HEADS):
            high = (hd // cfg.group) % 2 == 1
            qpad[hd] = jnp.concatenate([zq, qn[hd]] if high else [qn[hd], zq], axis=0)
        gts[...] = _silu(pT[:, pair_rows:2 * pair_rows]).reshape(N_HEADS, HEAD_DIM, ts)
        if cfg.kv_per_pair:
            vT = pT[:, 2 * pair_rows:3 * pair_rows].reshape(kvd, ts)
        else:
            vT = lax.dot_general(wvT_ref[...], h, NT_DIMS, preferred_element_type=F32)
        vnew[...] = vT.T.reshape(ns, tn, kvd)

    ck = ck_ref[...]
    cv = cv_ref[...]
    kn = knew[b]
    vn = vnew[b]
    newk_ref[0:hw - tn, :] = ck[tn:, :]
    newk_ref[hw - tn:, :] = kn
    newv_ref[0:hw - tn, :] = cv[tn:, :]
    newv_ref[hw - tn:, :] = vn
    for kb in range(kvd // LANES):
        kwin[b, kb, 0:hw, :] = ck[:, kb * LANES:(kb + 1) * LANES].astype(BF16)
        kwin[b, kb, hw:ws, :] = kn[:, kb * LANES:(kb + 1) * LANES].astype(BF16)
    vrows = jnp.concatenate([cv, vn, jnp.zeros((LANES - tn, kvd), F32)], axis=0)
    vwin[b] = vrows.T[:, :ws].reshape(cfg.n_kv, HEAD_DIM, ws).astype(BF16)

    @pl.when(b == ns - 1)
    def _attend():
        wlane = lax.broadcasted_iota(jnp.int32, (pair_rows, ts), 1) // tn
        wmask = [(wlane == i).astype(F32).astype(BF16) for i in range(ns)]
        olane = lax.broadcasted_iota(jnp.int32, (HEAD_DIM, ts), 1) // tn

        def scores(hd, slot):
            kb = (hd // cfg.group) // 2
            q = qpad[hd]
            wbd = jnp.concatenate([q * wmask[i] for i in range(ns)], axis=0)
            kcat = jnp.concatenate([kwin[i, kb] for i in range(ns)], axis=1)
            sbuf[slot] = jnp.dot(kcat, wbd, preferred_element_type=F32)

        def softmax(hd, slot):
            sink = sinks_ref[hd] if cfg.use_sinks else None
            for lt in range(ts // LANES):
                lanes = slice(lt * LANES, (lt + 1) * LANES)
                p, li = _softmax_cols(sbuf[slot, :, lanes] + bias_ref[hd], sink)
                pbuf[slot, :, lanes] = p.astype(BF16)
                lbuf[slot, :, lanes] = li

        def weighted_values(hd, slot):
            kvh = hd // cfg.group
            vcat = jnp.concatenate([vwin[i, kvh] for i in range(ns)], axis=0)
            r = jnp.dot(vcat, pbuf[slot], preferred_element_type=F32)
            o = r[0:HEAD_DIM]
            for i in range(1, ns):
                o = jnp.where(olane == i, r[i * HEAD_DIM:(i + 1) * HEAD_DIM], o)
            aT[hd] = (o * lbuf[slot] * gts[hd]).astype(BF16)

        _pipelined_heads(scores, softmax, weighted_values)

        yT = jnp.dot(woT_ref[...], aT[...].reshape(D_MODEL, ts), preferred_element_type=F32)
        y_ref[...] = x_ref[...] + yT.T.reshape(ns, tn, D_MODEL)


def _nbytes(shape, dtype):
    return math.prod(shape) * jnp.dtype(dtype).itemsize


def _vmem_limit(block_bytes, scratch_bytes, temp_bytes):
    est = 2 * block_bytes + scratch_bytes + temp_bytes
    return int(min(VMEM_BYTES - 4 * 2**20, max(est, 32 * 2**20)))


def _weight_operands(p, cfg):
    ops = [p["ng"], p["wk"]]
    if not cfg.kv_per_pair:
        ops.append(p["wvT"])
    return ops + [p["wsl"], p["woT"], p["gmat"], p["gq"], p["gk"]]


def _prompt_layer(x, p, cfg, tile):
    s = x.shape[0]
    hw, kvd = cfg.hist, cfg.kvd
    nq = tile // QUAD
    consts = _weight_operands(p, cfg) + [p["bias"]]
    operands = [x] + consts
    in_specs = [pl.BlockSpec((tile, D_MODEL), lambda t: (t, 0))] + [
        pl.BlockSpec(c.shape, lambda t, _nd=c.ndim: (0,) * _nd) for c in consts]
    if cfg.use_sinks:
        operands = [p["sinks"]] + operands
        in_specs = [pl.BlockSpec(memory_space=pltpu.SMEM)] + in_specs
    out_shape = (jax.ShapeDtypeStruct((s, D_MODEL), F32),
                 jax.ShapeDtypeStruct((hw, kvd), F32),
                 jax.ShapeDtypeStruct((hw, kvd), F32))
    keep_spec = pl.BlockSpec((hw, kvd), lambda t: (0, 0))
    out_specs = (pl.BlockSpec((tile, D_MODEL), lambda t: (t, 0)), keep_spec, keep_spec)
    scratch = [
        ((tile, D_MODEL), BF16),
        ((kvd // LANES, hw + tile, LANES), BF16),
        ((cfg.n_kv, HEAD_DIM, hw + tile), BF16),
        ((tile, kvd), F32),
        ((kvd, tile), F32),
        ((N_HEADS, 2 * HEAD_DIM, tile), BF16),
        ((N_HEADS, HEAD_DIM, tile), F32),
        ((N_HEADS, HEAD_DIM, tile), BF16),
        ((2 * nq, cfg.pair_win, LANES), F32),
        ((2, nq, cfg.quad_win, QUAD), F32),
        ((2, nq, cfg.quad_win, QUAD), BF16),
        ((2, nq, 1, QUAD), F32),
    ]
    block_bytes = (2 * _nbytes((tile, D_MODEL), F32) + 2 * _nbytes((hw, kvd), F32)
                   + sum(_nbytes(c.shape, c.dtype) for c in consts))
    scratch_bytes = sum(_nbytes(sh, dt) for sh, dt in scratch)
    temp_bytes = 6 * _nbytes((D_MODEL, tile), F32) + 4 * _nbytes((cfg.quad_win, QUAD), F32)
    return pl.pallas_call(
        functools.partial(_prompt_layer_kernel, cfg=cfg, tile=tile),
        grid=(s // tile,),
        in_specs=in_specs,
        out_specs=out_specs,
        out_shape=out_shape,
        scratch_shapes=[pltpu.VMEM(sh, dt) for sh, dt in scratch],
        compiler_params=pltpu.CompilerParams(
            dimension_semantics=("arbitrary",),
            vmem_limit_bytes=_vmem_limit(block_bytes, scratch_bytes, temp_bytes)),
        name=f"prompt_layer_kv{cfg.n_kv}",
    )(*operands)


def _sample_layer(x, ck, cv, p, cfg):
    b, tn, _ = x.shape
    hw, kvd = cfg.hist, cfg.kvd
    ns = SAMPLE_GROUP
    ts, ws = ns * tn, hw + tn
    assert tn == CHUNK and b % ns == 0
    bias_s = p["bias"][:, :ws, :tn]
    bias_s = jnp.concatenate([bias_s] * (LANES // tn), axis=2)
    consts = _weight_operands(p, cfg) + [bias_s]
    operands = [x, ck, cv] + consts
    seq_spec = pl.BlockSpec((None, hw, kvd), lambda g, i: (g * ns + i, 0, 0))
    grp_spec = pl.BlockSpec((ns, tn, D_MODEL), lambda g, i: (g, 0, 0))
    in_specs = [grp_spec, seq_spec, seq_spec] + [
        pl.BlockSpec(c.shape, lambda g, i, _nd=c.ndim: (0,) * _nd) for c in consts]
    if cfg.use_sinks:
        operands = [p["sinks"]] + operands
        in_specs = [pl.BlockSpec(memory_space=pltpu.SMEM)] + in_specs
    out_shape = (jax.ShapeDtypeStruct((b, tn, D_MODEL), F32),
                 jax.ShapeDtypeStruct((b, hw, kvd), F32),
                 jax.ShapeDtypeStruct((b, hw, kvd), F32))
    out_specs = (grp_spec, seq_spec, seq_spec)
    scratch = [
        ((ns, kvd // LANES, ws, LANES), BF16),
        ((ns, cfg.n_kv, HEAD_DIM, ws), BF16),
        ((ns, tn, kvd), F32),
        ((ns, tn, kvd), F32),
        ((N_HEADS, 2 * HEAD_DIM, ts), BF16),
        ((N_HEADS, HEAD_DIM, ts), F32),
        ((N_HEADS, HEAD_DIM, ts), BF16),
        ((2, ws, ts), F32),
        ((2, ws, ts), BF16),
        ((2, 1, ts), F32),
    ]
    block_bytes = (2 * _nbytes((ts, D_MODEL), F32) + 4 * _nbytes((hw, kvd), F32)
                   + sum(_nbytes(c.shape, c.dtype) for c in consts))
    scratch_bytes = sum(_nbytes(sh, dt) for sh, dt in scratch)
    temp_bytes = 6 * _nbytes((D_MODEL, ts), F32) + 3 * _nbytes((hw + LANES, kvd), F32)
    return pl.pallas_call(
        functools.partial(_sample_layer_kernel, cfg=cfg),
        grid=(b // ns, ns),
        in_specs=in_specs,
        out_specs=out_specs,
        out_shape=out_shape,
        scratch_shapes=[pltpu.VMEM(sh, dt) for sh, dt in scratch],
        compiler_params=pltpu.CompilerParams(
            dimension_semantics=("arbitrary", "arbitrary"),
            vmem_limit_bytes=_vmem_limit(block_bytes, scratch_bytes, temp_bytes)),
        name=f"sample_layer_kv{cfg.n_kv}",
    )(*operands)


def _t5_bucket(rel):
    nb = T5_BUCKETS // 2
    max_exact = nb // 2
    ret = jnp.where(rel > 0, nb, 0)
    n = jnp.abs(rel)
    nf = jnp.maximum(n, 1).astype(F32)
    large = max_exact + (jnp.log(nf / max_exact) / math.log(T5_MAX_DIST / max_exact)
                         * (nb - max_exact)).astype(jnp.int32)
    large = jnp.minimum(large, nb - 1)
    return ret + jnp.where(n < max_exact, n, large)


def _pair_bias(cfg, bias_of_offset):
    hw, wp = cfg.hist, cfg.pair_win
    row0 = max(hw - BIAS_SATURATION, 0)
    band = wp - row0
    period = band + PAIR
    idx = np.arange(period)
    offset = hw - row0 + np.where(idx < PAIR, idx, idx - period)
    gen = bias_of_offset(jnp.asarray(offset, jnp.int32)).astype(F32)
    flat = jnp.tile(gen, (1, band))[:, :band * (period - 1)]
    b = flat.reshape(N_HEADS, band, period - 1)[:, :, :PAIR]
    far = bias_of_offset(jnp.full((1,), BIAS_SATURATION, jnp.int32)).astype(F32)
    b = jnp.concatenate([jnp.broadcast_to(far[:, :, None], (N_HEADS, row0, PAIR)), b], axis=1)
    qc = (hw + np.arange(PAIR)) // CHUNK
    kc = np.arange(wp) // CHUNK
    valid = (kc[:, None] <= qc[None, :]) & (kc[:, None] >= qc[None, :] - cfg.past_chunks)
    return jnp.where(jnp.asarray(valid)[None], b, NEG_INF)


def _prep_layer(cfg, norm_g, w_in, q_g, k_g, w_out, bias, sinks):
    wq, wkv = N_HEADS * HEAD_DIM, cfg.kvd
    pair_rows = 2 * HEAD_DIM
    gn = cfg.gnorm
    blk = jnp.arange(gn) // HEAD_DIM
    w_inT = w_in.T.astype(BF16)
    parts = [w_inT[:wq].reshape(N_PAIRS, pair_rows, D_MODEL),
             w_inT[wq + 2 * wkv:].reshape(N_PAIRS, pair_rows, D_MODEL)]
    if cfg.kv_per_pair:
        parts.append(w_inT[wq + wkv:wq + 2 * wkv].reshape(N_PAIRS, pair_rows, D_MODEL))
    p = {
        "ng": norm_g[None, :],
        "wk": w_in[:, wq:wq + wkv].astype(BF16),
        "wsl": jnp.concatenate(parts, axis=1),
        "woT": w_out.T.astype(BF16),
        "gmat": (blk[:, None] == blk[None, :]).astype(BF16),
        "gq": jnp.broadcast_to((q_g * (HEAD_DIM ** -0.5 * LOG2E))[:, None], (HEAD_DIM, LANES)).astype(F32),
        "gk": jnp.tile(k_g, cfg.n_kv)[None, :],
        "bias": bias * LOG2E,
    }
    if not cfg.kv_per_pair:
        p["wvT"] = w_inT[wq + wkv:wq + 2 * wkv]
    if cfg.use_sinks:
        p["sinks"] = sinks.astype(F32) * LOG2E
    return p


def _prompt_tile(cfg):
    return QUAD


def kernel(x_prompt, x_sample, cache_k_a, cache_v_a, cache_k_b, cache_v_b, t5_table, norm_a, w_in_a, q_norm_a, k_norm_a, sinks_a, w_out_a, norm_b, w_in_b, q_norm_b, k_norm_b, rel_bias_b, w_out_b):
    bsz, seq, _ = x_prompt.shape
    assert bsz == 1 and seq % QUAD == 0
    db = x_sample.shape[0]

    def t5_fn(offset):
        return t5_table[_t5_bucket(-offset)].T

    def rel_fn(offset):
        return rel_bias_b[:, jnp.clip(offset, -REL_CLIP, REL_CLIP) + REL_CLIP]

    pa = _prep_layer(CFG_A, norm_a, w_in_a, q_norm_a, k_norm_a, w_out_a,
                     _pair_bias(CFG_A, t5_fn), sinks_a)
    pb = _prep_layer(CFG_B, norm_b, w_in_b, q_norm_b, k_norm_b, w_out_b,
                     _pair_bias(CFG_B, rel_fn), None)

    xp = x_prompt.reshape(seq, D_MODEL)
    yp_a, k_a_p, v_a_p = _prompt_layer(xp, pa, CFG_A, _prompt_tile(CFG_A))
    yp_b, k_b_p, v_b_p = _prompt_layer(yp_a, pb, CFG_B, _prompt_tile(CFG_B))

    ys_a, k_a_s, v_a_s = _sample_layer(
        x_sample, cache_k_a.reshape(db, CFG_A.hist, CFG_A.kvd),
        cache_v_a.reshape(db, CFG_A.hist, CFG_A.kvd), pa, CFG_A)
    ys_b, k_b_s, v_b_s = _sample_layer(
        ys_a, cache_k_b.reshape(db, CFG_B.hist, CFG_B.kvd),
        cache_v_b.reshape(db, CFG_B.hist, CFG_B.kvd), pb, CFG_B)

    return (yp_b.reshape(1, seq, D_MODEL), ys_b,
            k_a_p.reshape(1, CFG_A.hist, CFG_A.n_kv, HEAD_DIM),
            v_a_p.reshape(1, CFG_A.hist, CFG_A.n_kv, HEAD_DIM),
            k_b_p.reshape(1, CFG_B.hist, CFG_B.n_kv, HEAD_DIM),
            v_b_p.reshape(1, CFG_B.hist, CFG_B.n_kv, HEAD_DIM),
            k_a_s.reshape(db, CFG_A.hist, CFG_A.n_kv, HEAD_DIM),
            v_a_s.reshape(db, CFG_A.hist, CFG_A.n_kv, HEAD_DIM),
            k_b_s.reshape(db, CFG_B.hist, CFG_B.n_kv, HEAD_DIM),
            v_b_s.reshape(db, CFG_B.hist, CFG_B.n_kv, HEAD_DIM))
```

```python
import functools
import math
from typing import NamedTuple

import jax
import jax.numpy as jnp
import numpy as np
from jax import lax
from jax.experimental import pallas as pl
from jax.experimental.pallas import tpu as pltpu

D_MODEL = 1024
CHUNK = 64
HEAD_DIM = 64
N_HEADS = 16
N_PAIRS = N_HEADS // 2
REL_CLIP = 128
T5_BUCKETS = 32
T5_MAX_DIST = 128
BIAS_SATURATION = max(REL_CLIP, T5_MAX_DIST)
EPS = 1e-6
NEG_INF = -1e30
LOG2E = math.log2(math.e)

LANES = 128
QUAD = 256
PAIR = LANES
SAMPLE_GROUP = QUAD // CHUNK
MXU_DIM = 256
VMEM_BYTES = 64 * 2**20

BF16 = jnp.bfloat16
F32 = jnp.float32
NT_DIMS = (((1,), (1,)), ((), ()))


class LayerCfg(NamedTuple):
    n_kv: int
    past_chunks: int
    use_sinks: bool

    @property
    def kvd(self):
        return self.n_kv * HEAD_DIM

    @property
    def group(self):
        return N_HEADS // self.n_kv

    @property
    def hist(self):
        return self.past_chunks * CHUNK

    @property
    def pair_win(self):
        return self.hist + PAIR

    @property
    def quad_win(self):
        return self.hist + QUAD

    @property
    def gnorm(self):
        return min(self.kvd, MXU_DIM)

    @property
    def kv_per_pair(self):
        return self.group == 1

    @property
    def slice_rows(self):
        return (3 if self.kv_per_pair else 2) * 2 * HEAD_DIM


CFG_A = LayerCfg(n_kv=2, past_chunks=2, use_sinks=True)
CFG_B = LayerCfg(n_kv=16, past_chunks=8, use_sinks=False)


def _rms_rows(x, g_row):
    ms = jnp.mean(x * x, axis=-1, keepdims=True)
    return x * lax.rsqrt(ms + EPS) * g_row


def _project_k(h, wk_ref, gmat_ref, gk_ref, cfg):
    k = jnp.dot(h, wk_ref[...], preferred_element_type=F32)
    ksq = (k * k).astype(BF16)
    gm = gmat_ref[...]
    gn = cfg.gnorm
    ss = jnp.concatenate(
        [jnp.dot(ksq[:, c:c + gn], gm, preferred_element_type=F32)
         for c in range(0, cfg.kvd, gn)], axis=1)
    return k * lax.rsqrt(ss * (1.0 / HEAD_DIM) + EPS) * gk_ref[...]


def _norm_q(q3, gq_ref):
    t = q3.shape[-1]
    ms = jnp.mean(q3 * q3, axis=1, keepdims=True)
    g = gq_ref[...][:, :t] if t < LANES else jnp.concatenate([gq_ref[...]] * (t // LANES), axis=1)
    return (q3 * lax.rsqrt(ms + EPS) * g[None]).astype(BF16)


def _silu(x):
    return x * jax.nn.sigmoid(x)


def _softmax_cols(s, sink):
    m = jnp.max(s, axis=0, keepdims=True)
    if sink is not None:
        m = jnp.maximum(m, sink)
    p = jnp.exp2(s - m)
    l = jnp.sum(p, axis=0, keepdims=True)
    if sink is not None:
        l = l + jnp.exp2(sink - m)
    return p, 1.0 / l


def _pipelined_heads(scores, softmax, weighted_values, prepare_pair=None):
    if prepare_pair is not None:
        prepare_pair(0)
        prepare_pair(1)
    scores(0, 0)
    scores(1, 1)
    softmax(0, 0)

    def head_pair(j):
        odd = 2 * j + 1
        scores(odd + 1, 0)
        weighted_values(odd - 1, 0)
        softmax(odd, 1)
        scores(odd + 2, 1)
        weighted_values(odd, 1)
        softmax(odd + 1, 0)

    def trip(j, carry):
        if prepare_pair is not None:
            prepare_pair(j + 2)
        head_pair(j)
        return carry

    if prepare_pair is None:
        lax.fori_loop(0, N_PAIRS - 1, trip, 0)
    else:
        for j in range(N_PAIRS - 2):
            trip(j, 0)
        head_pair(N_PAIRS - 2)
    weighted_values(N_HEADS - 2, 0)
    softmax(N_HEADS - 1, 1)
    weighted_values(N_HEADS - 1, 1)


def _prompt_layer_kernel(*refs, cfg, tile):
    refs = list(refs)
    sinks_ref = refs.pop(0) if cfg.use_sinks else None
    x_ref, ng_ref, wk_ref = refs[:3]
    refs = refs[3:]
    wvT_ref = None if cfg.kv_per_pair else refs.pop(0)
    (wsl_ref, woT_ref, gmat_ref, gq_ref, gk_ref, bias_ref, y_ref, kkeep_ref, vkeep_ref,
     hbuf, xprev, khist, vhist, klast, vlast, qpad, gts, aT, maskbuf, sbuf, pbuf, lbuf) = refs
    t = pl.program_id(0)
    nt = pl.num_programs(0) - 1
    hw, wp, w = cfg.hist, cfg.pair_win, cfg.quad_win
    nq = tile // QUAD
    pair_rows = 2 * HEAD_DIM

    def finish_previous_tile():
        yT = jnp.dot(woT_ref[...], aT[...].reshape(D_MODEL, tile), preferred_element_type=F32)
        y_ref[...] = xprev[...] + yT.T

    def attend_tile():
        n_invalid = jnp.maximum(hw - t * tile, 0)
        rows = lax.broadcasted_iota(jnp.int32, (wp, LANES), 0)
        for qi in range(nq):
            for lt in range(2):
                off = qi * QUAD + lt * PAIR
                maskbuf[2 * qi + lt] = jnp.where(rows + off < n_invalid, NEG_INF, 0.0).astype(F32)

        x = x_ref[...]
        h = _rms_rows(x, ng_ref[...]).astype(BF16)
        hbuf[...] = h

        kn = _project_k(h, wk_ref, gmat_ref, gk_ref, cfg)
        klast[...] = kn
        for kb in range(cfg.kvd // LANES):
            khist[kb, hw:hw + tile, :] = kn[:, kb * LANES:(kb + 1) * LANES].astype(BF16)

        if not cfg.kv_per_pair:
            vT = lax.dot_general(wvT_ref[...], h, NT_DIMS, preferred_element_type=F32)
            vhist[:, :, hw:hw + tile] = vT.reshape(cfg.n_kv, HEAD_DIM, tile).astype(BF16)
            vlast[...] = vT

        zq = jnp.zeros((HEAD_DIM, tile), BF16)

        def prepare_pair(j):
            pT = lax.dot_general(wsl_ref[j], hbuf[...], NT_DIMS, preferred_element_type=F32)
            qn = _norm_q(pT[0:pair_rows].reshape(2, HEAD_DIM, tile), gq_ref)
            for i in range(2):
                high = ((2 * j + i) // cfg.group) % 2 == 1
                qpad[2 * j + i] = jnp.concatenate([zq, qn[i]] if high else [qn[i], zq], axis=0)
            gts[2 * j:2 * j + 2] = _silu(pT[pair_rows:2 * pair_rows]).reshape(2, HEAD_DIM, tile)
            if cfg.kv_per_pair:
                vT2 = pT[2 * pair_rows:3 * pair_rows]
                vhist[2 * j:2 * j + 2, :, hw:hw + tile] = vT2.reshape(2, HEAD_DIM, tile).astype(BF16)
                vlast[j * pair_rows:(j + 1) * pair_rows, :] = vT2

        def scores(hd, slot):
            kb = (hd // cfg.group) // 2
            for qi in range(nq):
                r0 = qi * QUAD
                sbuf[slot, qi] = jnp.dot(khist[kb, r0:r0 + w, :], qpad[hd, :, r0:r0 + QUAD],
                                         preferred_element_type=F32)

        def softmax(hd, slot):
            sink = sinks_ref[hd] if cfg.use_sinks else None
            for qi in range(nq):
                for lt in range(2):
                    lanes = slice(lt * PAIR, (lt + 1) * PAIR)
                    keys = slice(lt * PAIR, lt * PAIR + wp)
                    s = sbuf[slot, qi, keys, lanes] + bias_ref[hd] + maskbuf[2 * qi + lt]
                    p, li = _softmax_cols(s, sink)
                    pbuf[slot, qi, keys, lanes] = p.astype(BF16)
                    lbuf[slot, qi, :, lanes] = li

        def weighted_values(hd, slot):
            kvh = hd // cfg.group
            for qi in range(nq):
                r0 = qi * QUAD
                oT = jnp.dot(vhist[kvh, :, r0:r0 + w], pbuf[slot, qi], preferred_element_type=F32)
                o = oT * lbuf[slot, qi]
                aT[hd, :, r0:r0 + QUAD] = (o * gts[hd, :, r0:r0 + QUAD]).astype(BF16)

        _pipelined_heads(scores, softmax, weighted_values, prepare_pair)
        xprev[...] = x

    @pl.when(t == 0)
    def _init():
        khist[:, :hw, :] = jnp.zeros((khist.shape[0], hw, LANES), BF16)
        vhist[:, :, :hw] = jnp.zeros((cfg.n_kv, HEAD_DIM, hw), BF16)
        zpad = jnp.zeros((2, nq, PAIR, LANES), BF16)
        pbuf[:, :, wp:w, 0:PAIR] = zpad
        pbuf[:, :, 0:PAIR, PAIR:QUAD] = zpad
        aT[...] = jnp.zeros(aT.shape, BF16)
        xprev[...] = jnp.zeros(xprev.shape, F32)

    @pl.when(jnp.logical_and(t > 0, t < nt))
    def _roll_history():
        khist[:, 0:hw, :] = khist[:, tile:tile + hw, :]
        vhist[:, :, 0:hw] = vhist[:, :, tile:tile + hw]

    @pl.when(t < nt)
    def _tile():
        finish_previous_tile()
        attend_tile()

    @pl.when(t == nt)
    def _finish_last_tile():
        finish_previous_tile()

    if tile >= hw:
        @pl.when(t == nt - 1)
        def _keep():
            kkeep_ref[...] = klast[tile - hw:, :]
            vkeep_ref[...] = vlast[:, tile - hw:].T
    else:
        @pl.when(jnp.logical_and(t >= nt - hw // tile, t < nt))
        def _keep():
            row0 = pl.multiple_of(hw - (nt - t) * tile, tile)
            kkeep_ref[pl.ds(row0, tile), :] = klast[...]
            vkeep_ref[pl.ds(row0, tile), :] = vlast[...].T


def _sample_layer_kernel(*refs, cfg):
    refs = list(refs)
    sinks_ref = refs.pop(0) if cfg.use_sinks else None
    x_ref, ck_ref, cv_ref, ng_ref, wk_ref = refs[:5]
    refs = refs[5:]
    wvT_ref = None if cfg.kv_per_pair else refs.pop(0)
    (wsl_ref, woT_ref, gmat_ref, gq_ref, gk_ref, bias_ref, y_ref, newk_ref, newv_ref,
     kwin, vwin, knew, vnew, qpad, gts, aT, sbuf, pbuf, lbuf) = refs
    b = pl.program_id(1)
    hw, kvd = cfg.hist, cfg.kvd
    tn, ns = CHUNK, SAMPLE_GROUP
    ts, ws = ns * tn, hw + tn
    pair_rows = 2 * HEAD_DIM

    @pl.when(b == 0)
    def _project():
        x = x_ref[...].reshape(ts, D_MODEL)
        h = _rms_rows(x, ng_ref[...]).astype(BF16)
        knew[...] = _project_k(h, wk_ref, gmat_ref, gk_ref, cfg).reshape(ns, tn, kvd)
        wsl = wsl_ref[...].reshape(N_PAIRS * cfg.slice_rows, D_MODEL)
        pT = lax.dot_general(wsl, h, NT_DIMS, preferred_element_type=F32)
        pT = pT.reshape(N_PAIRS, cfg.slice_rows, ts)
        qn = _norm_q(pT[:, 0:pair_rows].reshape(N_HEADS, HEAD_DIM, ts), gq_ref)
        zq = jnp.zeros((HEAD_DIM, ts), BF16)
        for hd in range(N_HEADS):
            high = (hd // cfg.group) % 2 == 1
            qpad[hd] = jnp.concatenate([zq, qn[hd]] if high else [qn[hd], zq], axis=0)
        gts[...] = _silu(pT[:, pair_rows:2 * pair_rows]).reshape(N_HEADS, HEAD_DIM, ts)
        if cfg.kv_per_pair:
            vT = pT[:, 2 * pair_rows:3 * pair_rows].reshape(kvd, ts)
        else:
            vT = lax.dot_general(wvT_ref[...], h, NT_DIMS, preferred_element_type=F32)
        vnew[...] = vT.T.reshape(ns, tn, kvd)

    ck = ck_ref[...]
    cv = cv_ref[...]
    kn = knew[b]
    vn = vnew[b]
    newk_ref[0:hw - tn, :] = ck[tn:, :]
    newk_ref[hw - tn:, :] = kn
    newv_ref[0:hw - tn, :] = cv[tn:, :]
    newv_ref[hw - tn:, :] = vn
    for kb in range(kvd // LANES):
        kwin[b, kb, 0:hw, :] = ck[:, kb * LANES:(kb + 1) * LANES].astype(BF16)
        kwin[b, kb, hw:ws, :] = kn[:, kb * LANES:(kb + 1) * LANES].astype(BF16)
    vrows = jnp.concatenate([cv, vn, jnp.zeros((LANES - tn, kvd), F32)], axis=0)
    vwin[b] = vrows.T[:, :ws].reshape(cfg.n_kv, HEAD_DIM, ws).astype(BF16)

    @pl.when(b == ns - 1)
    def _attend():
        wlane = lax.broadcasted_iota(jnp.int32, (pair_rows, ts), 1) // tn
        wmask = [(wlane == i).astype(F32).astype(BF16) for i in range(ns)]
        olane = lax.broadcasted_iota(jnp.int32, (HEAD_DIM, ts), 1) // tn

        def scores(hd, slot):
            kb = (hd // cfg.group) // 2
            q = qpad[hd]
            wbd = jnp.concatenate([q * wmask[i] for i in range(ns)], axis=0)
            kcat = jnp.concatenate([kwin[i, kb] for i in range(ns)], axis=1)
            sbuf[slot] = jnp.dot(kcat, wbd, preferred_element_type=F32)

        def softmax(hd, slot):
            sink = sinks_ref[hd] if cfg.use_sinks else None
            for lt in range(ts // LANES):
                lanes = slice(lt * LANES, (lt + 1) * LANES)
                p, li = _softmax_cols(sbuf[slot, :, lanes] + bias_ref[hd], sink)
                pbuf[slot, :, lanes] = p.astype(BF16)
                lbuf[slot, :, lanes] = li

        def weighted_values(hd, slot):
            kvh = hd // cfg.group
            vcat = jnp.concatenate([vwin[i, kvh] for i in range(ns)], axis=0)
            r = jnp.dot(vcat, pbuf[slot], preferred_element_type=F32)
            o = r[0:HEAD_DIM]
            for i in range(1, ns):
                o = jnp.where(olane == i, r[i * HEAD_DIM:(i + 1) * HEAD_DIM], o)
            aT[hd] = (o * lbuf[slot] * gts[hd]).astype(BF16)

        _pipelined_heads(scores, softmax, weighted_values)

        yT = jnp.dot(woT_ref[...], aT[...].reshape(D_MODEL, ts), preferred_element_type=F32)
        y_ref[...] = x_ref[...] + yT.T.reshape(ns, tn, D_MODEL)


def _nbytes(shape, dtype):
    return math.prod(shape) * jnp.dtype(dtype).itemsize


def _vmem_limit(block_bytes, scratch_bytes, temp_bytes):
    est = 2 * block_bytes + scratch_bytes + temp_bytes
    return int(min(VMEM_BYTES - 4 * 2**20, max(est, 32 * 2**20)))


def _weight_operands(p, cfg):
    ops = [p["ng"], p["wk"]]
    if not cfg.kv_per_pair:
        ops.append(p["wvT"])
    return ops + [p["wsl"], p["woT"], p["gmat"], p["gq"], p["gk"]]


def _prompt_layer(x, p, cfg, tile):
    s = x.shape[0]
    hw, kvd = cfg.hist, cfg.kvd
    nq = tile // QUAD
    consts = _weight_operands(p, cfg) + [p["bias"]]
    operands = [x] + consts
    nt = s // tile
    in_specs = [pl.BlockSpec((tile, D_MODEL), lambda t: (jnp.minimum(t, nt - 1), 0))] + [
        pl.BlockSpec(c.shape, lambda t, _nd=c.ndim: (0,) * _nd) for c in consts]
    if cfg.use_sinks:
        operands = [p["sinks"]] + operands
        in_specs = [pl.BlockSpec(memory_space=pltpu.SMEM)] + in_specs
    out_shape = (jax.ShapeDtypeStruct((s, D_MODEL), F32),
                 jax.ShapeDtypeStruct((hw, kvd), F32),
                 jax.ShapeDtypeStruct((hw, kvd), F32))
    keep_spec = pl.BlockSpec((hw, kvd), lambda t: (0, 0))
    out_specs = (pl.BlockSpec((tile, D_MODEL), lambda t: (jnp.maximum(t - 1, 0), 0)), keep_spec, keep_spec)
    scratch = [
        ((tile, D_MODEL), BF16),
        ((tile, D_MODEL), F32),
        ((kvd // LANES, hw + tile, LANES), BF16),
        ((cfg.n_kv, HEAD_DIM, hw + tile), BF16),
        ((tile, kvd), F32),
        ((kvd, tile), F32),
        ((N_HEADS, 2 * HEAD_DIM, tile), BF16),
        ((N_HEADS, HEAD_DIM, tile), F32),
        ((N_HEADS, HEAD_DIM, tile), BF16),
        ((2 * nq, cfg.pair_win, LANES), F32),
        ((2, nq, cfg.quad_win, QUAD), F32),
        ((2, nq, cfg.quad_win, QUAD), BF16),
        ((2, nq, 1, QUAD), F32),
    ]
    block_bytes = (2 * _nbytes((tile, D_MODEL), F32) + 2 * _nbytes((hw, kvd), F32)
                   + sum(_nbytes(c.shape, c.dtype) for c in consts))
    scratch_bytes = sum(_nbytes(sh, dt) for sh, dt in scratch)
    temp_bytes = 6 * _nbytes((D_MODEL, tile), F32) + 4 * _nbytes((cfg.quad_win, QUAD), F32)
    return pl.pallas_call(
        functools.partial(_prompt_layer_kernel, cfg=cfg, tile=tile),
        grid=(nt + 1,),
        in_specs=in_specs,
        out_specs=out_specs,
        out_shape=out_shape,
        scratch_shapes=[pltpu.VMEM(sh, dt) for sh, dt in scratch],
        compiler_params=pltpu.CompilerParams(
            dimension_semantics=("arbitrary",),
            vmem_limit_bytes=_vmem_limit(block_bytes, scratch_bytes, temp_bytes)),
        name=f"prompt_layer_kv{cfg.n_kv}",
    )(*operands)


def _sample_layer(x, ck, cv, p, cfg):
    b, tn, _ = x.shape
    hw, kvd = cfg.hist, cfg.kvd
    ns = SAMPLE_GROUP
    ts, ws = ns * tn, hw + tn
    assert tn == CHUNK and b % ns == 0
    bias_s = p["bias"][:, :ws, :tn]
    bias_s = jnp.concatenate([bias_s] * (LANES // tn), axis=2)
    consts = _weight_operands(p, cfg) + [bias_s]
    operands = [x, ck, cv] + consts
    seq_spec = pl.BlockSpec((None, hw, kvd), lambda g, i: (g * ns + i, 0, 0))
    grp_spec = pl.BlockSpec((ns, tn, D_MODEL), lambda g, i: (g, 0, 0))
    in_specs = [grp_spec, seq_spec, seq_spec] + [
        pl.BlockSpec(c.shape, lambda g, i, _nd=c.ndim: (0,) * _nd) for c in consts]
    if cfg.use_sinks:
        operands = [p["sinks"]] + operands
        in_specs = [pl.BlockSpec(memory_space=pltpu.SMEM)] + in_specs
    out_shape = (jax.ShapeDtypeStruct((b, tn, D_MODEL), F32),
                 jax.ShapeDtypeStruct((b, hw, kvd), F32),
                 jax.ShapeDtypeStruct((b, hw, kvd), F32))
    out_specs = (grp_spec, seq_spec, seq_spec)
    scratch = [
        ((ns, kvd // LANES, ws, LANES), BF16),
        ((ns, cfg.n_kv, HEAD_DIM, ws), BF16),
        ((ns, tn, kvd), F32),
        ((ns, tn, kvd), F32),
        ((N_HEADS, 2 * HEAD_DIM, ts), BF16),
        ((N_HEADS, HEAD_DIM, ts), F32),
        ((N_HEADS, HEAD_DIM, ts), BF16),
        ((2, ws, ts), F32),
        ((2, ws, ts), BF16),
        ((2, 1, ts), F32),
    ]
    block_bytes = (2 * _nbytes((ts, D_MODEL), F32) + 4 * _nbytes((hw, kvd), F32)
                   + sum(_nbytes(c.shape, c.dtype) for c in consts))
    scratch_bytes = sum(_nbytes(sh, dt) for sh, dt in scratch)
    temp_bytes = 6 * _nbytes((D_MODEL, ts), F32) + 3 * _nbytes((hw + LANES, kvd), F32)
    return pl.pallas_call(
        functools.partial(_sample_layer_kernel, cfg=cfg),
        grid=(b // ns, ns),
        in_specs=in_specs,
        out_specs=out_specs,
        out_shape=out_shape,
        scratch_shapes=[pltpu.VMEM(sh, dt) for sh, dt in scratch],
        compiler_params=pltpu.CompilerParams(
            dimension_semantics=("arbitrary", "arbitrary"),
            vmem_limit_bytes=_vmem_limit(block_bytes, scratch_bytes, temp_bytes)),
        name=f"sample_layer_kv{cfg.n_kv}",
    )(*operands)


def _t5_bucket(rel):
    nb = T5_BUCKETS // 2
    max_exact = nb // 2
    ret = jnp.where(rel > 0, nb, 0)
    n = jnp.abs(rel)
    nf = jnp.maximum(n, 1).astype(F32)
    large = max_exact + (jnp.log(nf / max_exact) / math.log(T5_MAX_DIST / max_exact)
                         * (nb - max_exact)).astype(jnp.int32)
    large = jnp.minimum(large, nb - 1)
    return ret + jnp.where(n < max_exact, n, large)


def _pair_bias(cfg, bias_of_offset):
    hw, wp = cfg.hist, cfg.pair_win
    row0 = max(hw - BIAS_SATURATION, 0)
    band = wp - row0
    period = band + PAIR
    idx = np.arange(period)
    offset = hw - row0 + np.where(idx < PAIR, idx, idx - period)
    gen = bias_of_offset(jnp.asarray(offset, jnp.int32)).astype(F32)
    flat = jnp.tile(gen, (1, band))[:, :band * (period - 1)]
    b = flat.reshape(N_HEADS, band, period - 1)[:, :, :PAIR]
    far = bias_of_offset(jnp.full((1,), BIAS_SATURATION, jnp.int32)).astype(F32)
    b = jnp.concatenate([jnp.broadcast_to(far[:, :, None], (N_HEADS, row0, PAIR)), b], axis=1)
    qc = (hw + np.arange(PAIR)) // CHUNK
    kc = np.arange(wp) // CHUNK
    valid = (kc[:, None] <= qc[None, :]) & (kc[:, None] >= qc[None, :] - cfg.past_chunks)
    return jnp.where(jnp.asarray(valid)[None], b, NEG_INF)


def _prep_layer(cfg, norm_g, w_in, q_g, k_g, w_out, bias, sinks):
    wq, wkv = N_HEADS * HEAD_DIM, cfg.kvd
    pair_rows = 2 * HEAD_DIM
    gn = cfg.gnorm
    blk = jnp.arange(gn) // HEAD_DIM
    w_inT = w_in.T.astype(BF16)
    parts = [w_inT[:wq].reshape(N_PAIRS, pair_rows, D_MODEL),
             w_inT[wq + 2 * wkv:].reshape(N_PAIRS, pair_rows, D_MODEL)]
    if cfg.kv_per_pair:
        parts.append(w_inT[wq + wkv:wq + 2 * wkv].reshape(N_PAIRS, pair_rows, D_MODEL))
    p = {
        "ng": norm_g[None, :],
        "wk": w_in[:, wq:wq + wkv].astype(BF16),
        "wsl": jnp.concatenate(parts, axis=1),
        "woT": w_out.T.astype(BF16),
        "gmat": (blk[:, None] == blk[None, :]).astype(BF16),
        "gq": jnp.broadcast_to((q_g * (HEAD_DIM ** -0.5 * LOG2E))[:, None], (HEAD_DIM, LANES)).astype(F32),
        "gk": jnp.tile(k_g, cfg.n_kv)[None, :],
        "bias": bias * LOG2E,
    }
    if not cfg.kv_per_pair:
        p["wvT"] = w_inT[wq + wkv:wq + 2 * wkv]
    if cfg.use_sinks:
        p["sinks"] = sinks.astype(F32) * LOG2E
    return p


def _prompt_tile(cfg):
    return QUAD


def kernel(x_prompt, x_sample, cache_k_a, cache_v_a, cache_k_b, cache_v_b, t5_table, norm_a, w_in_a, q_norm_a, k_norm_a, sinks_a, w_out_a, norm_b, w_in_b, q_norm_b, k_norm_b, rel_bias_b, w_out_b):
    bsz, seq, _ = x_prompt.shape
    assert bsz == 1 and seq % QUAD == 0
    db = x_sample.shape[0]

    def t5_fn(offset):
        return t5_table[_t5_bucket(-offset)].T

    def rel_fn(offset):
        return rel_bias_b[:, jnp.clip(offset, -REL_CLIP, REL_CLIP) + REL_CLIP]

    pa = _prep_layer(CFG_A, norm_a, w_in_a, q_norm_a, k_norm_a, w_out_a,
                     _pair_bias(CFG_A, t5_fn), sinks_a)
    pb = _prep_layer(CFG_B, norm_b, w_in_b, q_norm_b, k_norm_b, w_out_b,
                     _pair_bias(CFG_B, rel_fn), None)

    xp = x_prompt.reshape(seq, D_MODEL)
    yp_a, k_a_p, v_a_p = _prompt_layer(xp, pa, CFG_A, _prompt_tile(CFG_A))
    yp_b, k_b_p, v_b_p = _prompt_layer(yp_a, pb, CFG_B, _prompt_tile(CFG_B))

    ys_a, k_a_s, v_a_s = _sample_layer(
        x_sample, cache_k_a.reshape(db, CFG_A.hist, CFG_A.kvd),
        cache_v_a.reshape(db, CFG_A.hist, CFG_A.kvd), pa, CFG_A)
    ys_b, k_b_s, v_b_s = _sample_layer(
        ys_a, cache_k_b.reshape(db, CFG_B.hist, CFG_B.kvd),
        cache_v_b.reshape(db, CFG_B.hist, CFG_B.kvd), pb, CFG_B)

    return (yp_b.reshape(1, seq, D_MODEL), ys_b,
            k_a_p.reshape(1, CFG_A.hist, CFG_A.n_kv, HEAD_DIM),
            v_a_p.reshape(1, CFG_A.hist, CFG_A.n_kv, HEAD_DIM),
            k_b_p.reshape(1, CFG_B.hist, CFG_B.n_kv, HEAD_DIM),
            v_b_p.reshape(1, CFG_B.hist, CFG_B.n_kv, HEAD_DIM),
            k_a_s.reshape(db, CFG_A.hist, CFG_A.n_kv, HEAD_DIM),
            v_a_s.reshape(db, CFG_A.hist, CFG_A.n_kv, HEAD_DIM),
            k_b_s.reshape(db, CFG_B.hist, CFG_B.n_kv, HEAD_DIM),
            v_b_s.reshape(db, CFG_B.hist, CFG_B.n_kv, HEAD_DIM))
```

```python
import functools
import math
from typing import NamedTuple

import jax
import jax.numpy as jnp
import numpy as np
from jax import lax
from jax.experimental import pallas as pl
from jax.experimental.pallas import tpu as pltpu

D_MODEL = 1024
CHUNK = 64
HEAD_DIM = 64
N_HEADS = 16
N_PAIRS = N_HEADS // 2
REL_CLIP = 128
T5_BUCKETS = 32
T5_MAX_DIST = 128
BIAS_SATURATION = max(REL_CLIP, T5_MAX_DIST)
EPS = 1e-6
NEG_INF = -1e30
LOG2E = math.log2(math.e)

LANES = 128
QUAD = 256
PAIR = LANES
SAMPLE_GROUP = QUAD // CHUNK
MXU_DIM = 256
VMEM_BYTES = 64 * 2**20

BF16 = jnp.bfloat16
F32 = jnp.float32
NT_DIMS = (((1,), (1,)), ((), ()))


class LayerCfg(NamedTuple):
    n_kv: int
    past_chunks: int
    use_sinks: bool

    @property
    def kvd(self):
        return self.n_kv * HEAD_DIM

    @property
    def group(self):
        return N_HEADS // self.n_kv

    @property
    def hist(self):
        return self.past_chunks * CHUNK

    @property
    def pair_win(self):
        return self.hist + PAIR

    @property
    def quad_win(self):
        return self.hist + QUAD

    @property
    def gnorm(self):
        return min(self.kvd, MXU_DIM)

    @property
    def kv_per_pair(self):
        return self.group == 1

    @property
    def slice_rows(self):
        return (3 if self.kv_per_pair else 2) * 2 * HEAD_DIM


CFG_A = LayerCfg(n_kv=2, past_chunks=2, use_sinks=True)
CFG_B = LayerCfg(n_kv=16, past_chunks=8, use_sinks=False)


def _rms_rows(x, g_row):
    ms = jnp.mean(x * x, axis=-1, keepdims=True)
    return x * lax.rsqrt(ms + EPS) * g_row


def _project_k(h, wk_ref, gmat_ref, gk_ref, cfg):
    k = jnp.dot(h, wk_ref[...], preferred_element_type=F32)
    ksq = (k * k).astype(BF16)
    gm = gmat_ref[...]
    gn = cfg.gnorm
    ss = jnp.concatenate(
        [jnp.dot(ksq[:, c:c + gn], gm, preferred_element_type=F32)
         for c in range(0, cfg.kvd, gn)], axis=1)
    return k * lax.rsqrt(ss * (1.0 / HEAD_DIM) + EPS) * gk_ref[...]


def _norm_q(q3, gq_ref):
    t = q3.shape[-1]
    ms = jnp.mean(q3 * q3, axis=1, keepdims=True)
    g = gq_ref[...][:, :t] if t < LANES else jnp.concatenate([gq_ref[...]] * (t // LANES), axis=1)
    return (q3 * lax.rsqrt(ms + EPS) * g[None]).astype(BF16)


def _silu(x):
    return x * jax.nn.sigmoid(x)


def _softmax_cols(s, sink):
    m = jnp.max(s, axis=0, keepdims=True)
    if sink is not None:
        m = jnp.maximum(m, sink)
    p = jnp.exp2(s - m)
    l = jnp.sum(p, axis=0, keepdims=True)
    if sink is not None:
        l = l + jnp.exp2(sink - m)
    return p, 1.0 / l


def _pipelined_heads(scores, softmax, weighted_values, prepare_pair=None):
    if prepare_pair is not None:
        prepare_pair(0)
        prepare_pair(1)
    scores(0, 0)
    scores(1, 1)
    softmax(0, 0)

    def head_pair(j):
        odd = 2 * j + 1
        scores(odd + 1, 0)
        weighted_values(odd - 1, 0)
        softmax(odd, 1)
        scores(odd + 2, 1)
        weighted_values(odd, 1)
        softmax(odd + 1, 0)

    def trip(j, carry):
        if prepare_pair is not None:
            prepare_pair(j + 2)
        head_pair(j)
        return carry

    if prepare_pair is None:
        lax.fori_loop(0, N_PAIRS - 1, trip, 0)
    else:
        for j in range(N_PAIRS - 2):
            trip(j, 0)
        head_pair(N_PAIRS - 2)
    weighted_values(N_HEADS - 2, 0)
    softmax(N_HEADS - 1, 1)
    weighted_values(N_HEADS - 1, 1)


def _prompt_layer_kernel(*refs, cfg, tile):
    refs = list(refs)
    sinks_ref = refs.pop(0) if cfg.use_sinks else None
    x_ref, ng_ref, wk_ref = refs[:3]
    refs = refs[3:]
    wvT_ref = None if cfg.kv_per_pair else refs.pop(0)
    (wsl_ref, woT_ref, gmat_ref, gq_ref, gk_ref, bias_ref, y_ref, kkeep_ref, vkeep_ref,
     hbuf, xprev, khist, vhist, klast, vlast, qpad, gts, aT, maskbuf, sbuf, pbuf, lbuf) = refs
    t = pl.program_id(0)
    nt = pl.num_programs(0) - 1
    hw, wp, w = cfg.hist, cfg.pair_win, cfg.quad_win
    nq = tile // QUAD
    pair_rows = 2 * HEAD_DIM

    def finish_previous_tile():
        yT = jnp.dot(woT_ref[...], aT[...].reshape(D_MODEL, tile), preferred_element_type=F32)
        y_ref[...] = xprev[...] + yT.T

    def attend_tile():
        n_invalid = jnp.maximum(hw - t * tile, 0)
        rows = lax.broadcasted_iota(jnp.int32, (wp, LANES), 0)
        for qi in range(nq):
            for lt in range(2):
                off = qi * QUAD + lt * PAIR
                maskbuf[2 * qi + lt] = jnp.where(rows + off < n_invalid, NEG_INF, 0.0).astype(F32)

        x = x_ref[...]
        h = _rms_rows(x, ng_ref[...]).astype(BF16)
        hbuf[...] = h

        kn = _project_k(h, wk_ref, gmat_ref, gk_ref, cfg)
        klast[...] = kn
        for kb in range(cfg.kvd // LANES):
            khist[kb, hw:hw + tile, :] = kn[:, kb * LANES:(kb + 1) * LANES].astype(BF16)

        if not cfg.kv_per_pair:
            vT = lax.dot_general(wvT_ref[...], h, NT_DIMS, preferred_element_type=F32)
            vhist[:, :, hw:hw + tile] = vT.reshape(cfg.n_kv, HEAD_DIM, tile).astype(BF16)
            vlast[...] = vT

        zq = jnp.zeros((HEAD_DIM, tile), BF16)

        def prepare_pair(j):
            pT = lax.dot_general(wsl_ref[j], hbuf[...], NT_DIMS, preferred_element_type=F32)
            qn = _norm_q(pT[0:pair_rows].reshape(2, HEAD_DIM, tile), gq_ref)
            for i in range(2):
                high = ((2 * j + i) // cfg.group) % 2 == 1
                qpad[2 * j + i] = jnp.concatenate([zq, qn[i]] if high else [qn[i], zq], axis=0)
            gts[2 * j:2 * j + 2] = _silu(pT[pair_rows:2 * pair_rows]).reshape(2, HEAD_DIM, tile)
            if cfg.kv_per_pair:
                vT2 = pT[2 * pair_rows:3 * pair_rows]
                vhist[2 * j:2 * j + 2, :, hw:hw + tile] = vT2.reshape(2, HEAD_DIM, tile).astype(BF16)
                vlast[j * pair_rows:(j + 1) * pair_rows, :] = vT2

        def scores(hd, slot):
            kb = (hd // cfg.group) // 2
            for qi in range(nq):
                r0 = qi * QUAD
                sbuf[slot, qi] = jnp.dot(khist[kb, r0:r0 + w, :], qpad[hd, :, r0:r0 + QUAD],
                                         preferred_element_type=F32)

        def softmax(hd, slot):
            sink = sinks_ref[hd] if cfg.use_sinks else None
            for qi in range(nq):
                for lt in range(2):
                    lanes = slice(lt * PAIR, (lt + 1) * PAIR)
                    keys = slice(lt * PAIR, lt * PAIR + wp)
                    s = sbuf[slot, qi, keys, lanes] + bias_ref[hd] + maskbuf[2 * qi + lt]
                    p, li = _softmax_cols(s, sink)
                    pbuf[slot, qi, keys, lanes] = p.astype(BF16)
                    lbuf[slot, qi, :, lanes] = li

        def weighted_values(hd, slot):
            kvh = hd // cfg.group
            for qi in range(nq):
                r0 = qi * QUAD
                oT = jnp.dot(vhist[kvh, :, r0:r0 + w], pbuf[slot, qi], preferred_element_type=F32)
                o = oT * lbuf[slot, qi]
                aT[hd, :, r0:r0 + QUAD] = (o * gts[hd, :, r0:r0 + QUAD]).astype(BF16)

        _pipelined_heads(scores, softmax, weighted_values, prepare_pair)
        xprev[...] = x

    @pl.when(t == 0)
    def _init():
        khist[:, :hw, :] = jnp.zeros((khist.shape[0], hw, LANES), BF16)
        vhist[:, :, :hw] = jnp.zeros((cfg.n_kv, HEAD_DIM, hw), BF16)
        zpad = jnp.zeros((2, nq, PAIR, LANES), BF16)
        pbuf[:, :, wp:w, 0:PAIR] = zpad
        pbuf[:, :, 0:PAIR, PAIR:QUAD] = zpad
        aT[...] = jnp.zeros(aT.shape, BF16)
        xprev[...] = jnp.zeros(xprev.shape, F32)

    @pl.when(jnp.logical_and(t > 0, t < nt))
    def _roll_history():
        khist[:, 0:hw, :] = khist[:, tile:tile + hw, :]
        vhist[:, :, 0:hw] = vhist[:, :, tile:tile + hw]

    @pl.when(t < nt)
    def _tile():
        finish_previous_tile()
        attend_tile()

    @pl.when(t == nt)
    def _finish_last_tile():
        finish_previous_tile()

    if tile >= hw:
        @pl.when(t == nt - 1)
        def _keep():
            kkeep_ref[...] = klast[tile - hw:, :]
            vkeep_ref[...] = vlast[:, tile - hw:].T
    else:
        @pl.when(jnp.logical_and(t >= nt - hw // tile, t < nt))
        def _keep():
            row0 = pl.multiple_of(hw - (nt - t) * tile, tile)
            kkeep_ref[pl.ds(row0, tile), :] = klast[...]
            vkeep_ref[pl.ds(row0, tile), :] = vlast[...].T


def _sample_layer_kernel(*refs, cfg):
    refs = list(refs)
    sinks_ref = refs.pop(0) if cfg.use_sinks else None
    x_ref, ck_ref, cv_ref, ng_ref, wk_ref = refs[:5]
    refs = refs[5:]
    wvT_ref = None if cfg.kv_per_pair else refs.pop(0)
    (wsl_ref, woT_ref, gmat_ref, gq_ref, gk_ref, bias_ref, y_ref, newk_ref, newv_ref,
     kwin, vwin, knew, vnew, qpad, gts, aT, sbuf, pbuf, lbuf) = refs
    b = pl.program_id(1)
    hw, kvd = cfg.hist, cfg.kvd
    tn, ns = CHUNK, SAMPLE_GROUP
    ts, ws = ns * tn, hw + tn
    pair_rows = 2 * HEAD_DIM

    @pl.when(b == 0)
    def _project():
        x = x_ref[...].reshape(ts, D_MODEL)
        h = _rms_rows(x, ng_ref[...]).astype(BF16)
        knew[...] = _project_k(h, wk_ref, gmat_ref, gk_ref, cfg).reshape(ns, tn, kvd)
        wsl = wsl_ref[...].reshape(N_PAIRS * cfg.slice_rows, D_MODEL)
        pT = lax.dot_general(wsl, h, NT_DIMS, preferred_element_type=F32)
        pT = pT.reshape(N_PAIRS, cfg.slice_rows, ts)
        qn = _norm_q(pT[:, 0:pair_rows].reshape(N_HEADS, HEAD_DIM, ts), gq_ref)
        zq = jnp.zeros((HEAD_DIM, ts), BF16)
        for hd in range(N_HEADS):
            high = (hd // cfg.group) % 2 == 1
            qpad[hd] = jnp.concatenate([zq, qn[hd]] if high else [qn[hd], zq], axis=0)
        gts[...] = _silu(pT[:, pair_rows:2 * pair_rows]).reshape(N_HEADS, HEAD_DIM, ts)
        if cfg.kv_per_pair:
            vT = pT[:, 2 * pair_rows:3 * pair_rows].reshape(kvd, ts)
        else:
            vT = lax.dot_general(wvT_ref[...], h, NT_DIMS, preferred_element_type=F32)
        vnew[...] = vT.T.reshape(ns, tn, kvd)

    ck = ck_ref[...]
    cv = cv_ref[...]
    kn = knew[b]
    vn = vnew[b]
    newk_ref[0:hw - tn, :] = ck[tn:, :]
    newk_ref[hw - tn:, :] = kn
    newv_ref[0:hw - tn, :] = cv[tn:, :]
    newv_ref[hw - tn:, :] = vn
    for kb in range(kvd // LANES):
        kwin[b, kb, 0:hw, :] = ck[:, kb * LANES:(kb + 1) * LANES].astype(BF16)
        kwin[b, kb, hw:ws, :] = kn[:, kb * LANES:(kb + 1) * LANES].astype(BF16)
    vrows = jnp.concatenate([cv, vn, jnp.zeros((LANES - tn, kvd), F32)], axis=0)
    vwin[b] = vrows.T[:, :ws].reshape(cfg.n_kv, HEAD_DIM, ws).astype(BF16)

    @pl.when(b == ns - 1)
    def _attend():
        wlane = lax.broadcasted_iota(jnp.int32, (pair_rows, ts), 1) // tn
        wmask = [(wlane == i).astype(F32).astype(BF16) for i in range(ns)]
        olane = lax.broadcasted_iota(jnp.int32, (HEAD_DIM, ts), 1) // tn

        def scores(hd, slot):
            kb = (hd // cfg.group) // 2
            q = qpad[hd]
            wbd = jnp.concatenate([q * wmask[i] for i in range(ns)], axis=0)
            kcat = jnp.concatenate([kwin[i, kb] for i in range(ns)], axis=1)
            sbuf[slot] = jnp.dot(kcat, wbd, preferred_element_type=F32)

        def softmax(hd, slot):
            sink = sinks_ref[hd] if cfg.use_sinks else None
            for lt in range(ts // LANES):
                lanes = slice(lt * LANES, (lt + 1) * LANES)
                p, li = _softmax_cols(sbuf[slot, :, lanes] + bias_ref[hd], sink)
                pbuf[slot, :, lanes] = p.astype(BF16)
                lbuf[slot, :, lanes] = li

        def weighted_values(hd, slot):
            kvh = hd // cfg.group
            vcat = jnp.concatenate([vwin[i, kvh] for i in range(ns)], axis=0)
            r = jnp.dot(vcat, pbuf[slot], preferred_element_type=F32)
            o = r[0:HEAD_DIM]
            for i in range(1, ns):
                o = jnp.where(olane == i, r[i * HEAD_DIM:(i + 1) * HEAD_DIM], o)
            aT[hd] = (o * lbuf[slot] * gts[hd]).astype(BF16)

        _pipelined_heads(scores, softmax, weighted_values)

        yT = jnp.dot(woT_ref[...], aT[...].reshape(D_MODEL, ts), preferred_element_type=F32)
        y_ref[...] = x_ref[...] + yT.T.reshape(ns, tn, D_MODEL)


def _nbytes(shape, dtype):
    return math.prod(shape) * jnp.dtype(dtype).itemsize


def _vmem_limit(block_bytes, scratch_bytes, temp_bytes):
    est = 2 * block_bytes + scratch_bytes + temp_bytes
    return int(min(VMEM_BYTES - 4 * 2**20, max(est, 32 * 2**20)))


def _weight_operands(p, cfg):
    ops = [p["ng"], p["wk"]]
    if not cfg.kv_per_pair:
        ops.append(p["wvT"])
    return ops + [p["wsl"], p["woT"], p["gmat"], p["gq"], p["gk"]]


def _prompt_layer(x, p, cfg, tile):
    s = x.shape[0]
    hw, kvd = cfg.hist, cfg.kvd
    nq = tile // QUAD
    consts = _weight_operands(p, cfg) + [p["bias"]]
    operands = [x] + consts
    nt = s // tile
    in_specs = [pl.BlockSpec((tile, D_MODEL), lambda t: (jnp.minimum(t, nt - 1), 0))] + [
        pl.BlockSpec(c.shape, lambda t, _nd=c.ndim: (0,) * _nd) for c in consts]
    if cfg.use_sinks:
        operands = [p["sinks"]] + operands
        in_specs = [pl.BlockSpec(memory_space=pltpu.SMEM)] + in_specs
    out_shape = (jax.ShapeDtypeStruct((s, D_MODEL), F32),
                 jax.ShapeDtypeStruct((hw, kvd), F32),
                 jax.ShapeDtypeStruct((hw, kvd), F32))
    keep_spec = pl.BlockSpec((hw, kvd), lambda t: (0, 0))
    out_specs = (pl.BlockSpec((tile, D_MODEL), lambda t: (jnp.maximum(t - 1, 0), 0)), keep_spec, keep_spec)
    scratch = [
        ((tile, D_MODEL), BF16),
        ((tile, D_MODEL), F32),
        ((kvd // LANES, hw + tile, LANES), BF16),
        ((cfg.n_kv, HEAD_DIM, hw + tile), BF16),
        ((tile, kvd), F32),
        ((kvd, tile), F32),
        ((N_HEADS, 2 * HEAD_DIM, tile), BF16),
        ((N_HEADS, HEAD_DIM, tile), F32),
        ((N_HEADS, HEAD_DIM, tile), BF16),
        ((2 * nq, cfg.pair_win, LANES), F32),
        ((2, nq, cfg.quad_win, QUAD), F32),
        ((2, nq, cfg.quad_win, QUAD), BF16),
        ((2, nq, 1, QUAD), F32),
    ]
    block_bytes = (2 * _nbytes((tile, D_MODEL), F32) + 2 * _nbytes((hw, kvd), F32)
                   + sum(_nbytes(c.shape, c.dtype) for c in consts))
    scratch_bytes = sum(_nbytes(sh, dt) for sh, dt in scratch)
    temp_bytes = 6 * _nbytes((D_MODEL, tile), F32) + 4 * _nbytes((cfg.quad_win, QUAD), F32)
    return pl.pallas_call(
        functools.partial(_prompt_layer_kernel, cfg=cfg, tile=tile),
        grid=(nt + 1,),
        in_specs=in_specs,
        out_specs=out_specs,
        out_shape=out_shape,
        scratch_shapes=[pltpu.VMEM(sh, dt) for sh, dt in scratch],
        compiler_params=pltpu.CompilerParams(
            dimension_semantics=("arbitrary",),
            vmem_limit_bytes=_vmem_limit(block_bytes, scratch_bytes, temp_bytes)),
        name=f"prompt_layer_kv{cfg.n_kv}",
    )(*operands)


def _sample_layer(x, ck, cv, p, cfg):
    b, tn, _ = x.shape
    hw, kvd = cfg.hist, cfg.kvd
    ns = SAMPLE_GROUP
    ts, ws = ns * tn, hw + tn
    assert tn == CHUNK and b % ns == 0
    consts = _weight_operands(p, cfg) + [p["bias_sample"]]
    operands = [x, ck, cv] + consts
    seq_spec = pl.BlockSpec((None, hw, kvd), lambda g, i: (g * ns + i, 0, 0))
    grp_spec = pl.BlockSpec((ns, tn, D_MODEL), lambda g, i: (g, 0, 0))
    in_specs = [grp_spec, seq_spec, seq_spec] + [
        pl.BlockSpec(c.shape, lambda g, i, _nd=c.ndim: (0,) * _nd) for c in consts]
    if cfg.use_sinks:
        operands = [p["sinks"]] + operands
        in_specs = [pl.BlockSpec(memory_space=pltpu.SMEM)] + in_specs
    out_shape = (jax.ShapeDtypeStruct((b, tn, D_MODEL), F32),
                 jax.ShapeDtypeStruct((b, hw, kvd), F32),
                 jax.ShapeDtypeStruct((b, hw, kvd), F32))
    out_specs = (grp_spec, seq_spec, seq_spec)
    scratch = [
        ((ns, kvd // LANES, ws, LANES), BF16),
        ((ns, cfg.n_kv, HEAD_DIM, ws), BF16),
        ((ns, tn, kvd), F32),
        ((ns, tn, kvd), F32),
        ((N_HEADS, 2 * HEAD_DIM, ts), BF16),
        ((N_HEADS, HEAD_DIM, ts), F32),
        ((N_HEADS, HEAD_DIM, ts), BF16),
        ((2, ws, ts), F32),
        ((2, ws, ts), BF16),
        ((2, 1, ts), F32),
    ]
    block_bytes = (2 * _nbytes((ts, D_MODEL), F32) + 4 * _nbytes((hw, kvd), F32)
                   + sum(_nbytes(c.shape, c.dtype) for c in consts))
    scratch_bytes = sum(_nbytes(sh, dt) for sh, dt in scratch)
    temp_bytes = 6 * _nbytes((D_MODEL, ts), F32) + 3 * _nbytes((hw + LANES, kvd), F32)
    return pl.pallas_call(
        functools.partial(_sample_layer_kernel, cfg=cfg),
        grid=(b // ns, ns),
        in_specs=in_specs,
        out_specs=out_specs,
        out_shape=out_shape,
        scratch_shapes=[pltpu.VMEM(sh, dt) for sh, dt in scratch],
        compiler_params=pltpu.CompilerParams(
            dimension_semantics=("arbitrary", "arbitrary"),
            vmem_limit_bytes=_vmem_limit(block_bytes, scratch_bytes, temp_bytes)),
        name=f"sample_layer_kv{cfg.n_kv}",
    )(*operands)


def _t5_bucket(rel):
    nb = T5_BUCKETS // 2
    max_exact = nb // 2
    ret = jnp.where(rel > 0, nb, 0)
    n = jnp.abs(rel)
    nf = jnp.maximum(n, 1).astype(F32)
    large = max_exact + (jnp.log(nf / max_exact) / math.log(T5_MAX_DIST / max_exact)
                         * (nb - max_exact)).astype(jnp.int32)
    large = jnp.minimum(large, nb - 1)
    return ret + jnp.where(n < max_exact, n, large)


def _bias_band(cfg):
    row0 = max(cfg.hist - BIAS_SATURATION, 0)
    band = cfg.pair_win - row0
    return row0, band, band + PAIR


def _bias_kernel(far_ref, gen_ref, pair_ref, samp_ref, *, cfg):
    hw, wp = cfg.hist, cfg.pair_win
    ws = hw + CHUNK
    row0, band, period = _bias_band(cfg)
    gen = jnp.broadcast_to(gen_ref[...], (band, period))
    b = pltpu.roll(gen, 0, 1, stride=1, stride_axis=0)[:, :PAIR]
    if row0:
        far = jnp.full((row0, PAIR), far_ref[pl.program_id(0)], F32)
        b = jnp.concatenate([far, b], axis=0)
    rows = lax.broadcasted_iota(jnp.int32, (wp, PAIR), 0)
    cols = lax.broadcasted_iota(jnp.int32, (wp, PAIR), 1)
    kc = rows // CHUNK
    qc = (hw + cols) // CHUNK
    b = jnp.where((kc <= qc) & (kc >= qc - cfg.past_chunks), b, NEG_INF)
    pair_ref[...] = b
    bs = b[:ws]
    first_seq = lax.broadcasted_iota(jnp.int32, (ws, PAIR), 1) < CHUNK
    samp_ref[...] = jnp.where(first_seq, bs, pltpu.roll(bs, CHUNK, 1))


def _bias_tables(cfg, bias_of_offset):
    hw, wp = cfg.hist, cfg.pair_win
    ws = hw + CHUNK
    row0, band, period = _bias_band(cfg)
    idx = np.arange(period)
    offset = hw - row0 + np.where(idx < PAIR, idx, idx - period)
    gen = bias_of_offset(jnp.asarray(offset, jnp.int32)).astype(F32) * LOG2E
    far = bias_of_offset(jnp.full((1,), BIAS_SATURATION, jnp.int32)).astype(F32)[:, 0] * LOG2E
    return pl.pallas_call(
        functools.partial(_bias_kernel, cfg=cfg),
        grid=(N_HEADS,),
        in_specs=[pl.BlockSpec(memory_space=pltpu.SMEM),
                  pl.BlockSpec((None, 1, period), lambda h: (h, 0, 0))],
        out_specs=(pl.BlockSpec((None, wp, PAIR), lambda h: (h, 0, 0)),
                   pl.BlockSpec((None, ws, PAIR), lambda h: (h, 0, 0))),
        out_shape=(jax.ShapeDtypeStruct((N_HEADS, wp, PAIR), F32),
                   jax.ShapeDtypeStruct((N_HEADS, ws, PAIR), F32)),
        compiler_params=pltpu.CompilerParams(dimension_semantics=("arbitrary",)),
        name=f"bias_tables_kv{cfg.n_kv}",
    )(far, gen[:, None, :])


def _prep_layer(cfg, norm_g, w_in, q_g, k_g, w_out, bias_of_offset, sinks):
    wq, wkv = N_HEADS * HEAD_DIM, cfg.kvd
    pair_rows = 2 * HEAD_DIM
    gn = cfg.gnorm
    blk = jnp.arange(gn) // HEAD_DIM
    bias, bias_sample = _bias_tables(cfg, bias_of_offset)
    w_inT = w_in.T.astype(BF16)
    parts = [w_inT[:wq].reshape(N_PAIRS, pair_rows, D_MODEL),
             w_inT[wq + 2 * wkv:].reshape(N_PAIRS, pair_rows, D_MODEL)]
    if cfg.kv_per_pair:
        parts.append(w_inT[wq + wkv:wq + 2 * wkv].reshape(N_PAIRS, pair_rows, D_MODEL))
    p = {
        "ng": norm_g[None, :],
        "wk": w_in[:, wq:wq + wkv].astype(BF16),
        "wsl": jnp.concatenate(parts, axis=1),
        "woT": w_out.T.astype(BF16),
        "gmat": (blk[:, None] == blk[None, :]).astype(BF16),
        "gq": jnp.broadcast_to((q_g * (HEAD_DIM ** -0.5 * LOG2E))[:, None], (HEAD_DIM, LANES)).astype(F32),
        "gk": jnp.tile(k_g, cfg.n_kv)[None, :],
        "bias": bias,
        "bias_sample": bias_sample,
    }
    if not cfg.kv_per_pair:
        p["wvT"] = w_inT[wq + wkv:wq + 2 * wkv]
    if cfg.use_sinks:
        p["sinks"] = sinks.astype(F32) * LOG2E
    return p


def _prompt_tile(cfg):
    return QUAD


def kernel(x_prompt, x_sample, cache_k_a, cache_v_a, cache_k_b, cache_v_b, t5_table, norm_a, w_in_a, q_norm_a, k_norm_a, sinks_a, w_out_a, norm_b, w_in_b, q_norm_b, k_norm_b, rel_bias_b, w_out_b):
    bsz, seq, _ = x_prompt.shape
    assert bsz == 1 and seq % QUAD == 0
    db = x_sample.shape[0]

    def t5_fn(offset):
        return t5_table[_t5_bucket(-offset)].T

    def rel_fn(offset):
        return rel_bias_b[:, jnp.clip(offset, -REL_CLIP, REL_CLIP) + REL_CLIP]

    pa = _prep_layer(CFG_A, norm_a, w_in_a, q_norm_a, k_norm_a, w_out_a, t5_fn, sinks_a)
    pb = _prep_layer(CFG_B, norm_b, w_in_b, q_norm_b, k_norm_b, w_out_b, rel_fn, None)

    xp = x_prompt.reshape(seq, D_MODEL)
    yp_a, k_a_p, v_a_p = _prompt_layer(xp, pa, CFG_A, _prompt_tile(CFG_A))
    yp_b, k_b_p, v_b_p = _prompt_layer(yp_a, pb, CFG_B, _prompt_tile(CFG_B))

    ys_a, k_a_s, v_a_s = _sample_layer(
        x_sample, cache_k_a.reshape(db, CFG_A.hist, CFG_A.kvd),
        cache_v_a.reshape(db, CFG_A.hist, CFG_A.kvd), pa, CFG_A)
    ys_b, k_b_s, v_b_s = _sample_layer(
        ys_a, cache_k_b.reshape(db, CFG_B.hist, CFG_B.kvd),
        cache_v_b.reshape(db, CFG_B.hist, CFG_B.kvd), pb, CFG_B)

    return (yp_b.reshape(1, seq, D_MODEL), ys_b,
            k_a_p.reshape(1, CFG_A.hist, CFG_A.n_kv, HEAD_DIM),
            v_a_p.reshape(1, CFG_A.hist, CFG_A.n_kv, HEAD_DIM),
            k_b_p.reshape(1, CFG_B.hist, CFG_B.n_kv, HEAD_DIM),
            v_b_p.reshape(1, CFG_B.hist, CFG_B.n_kv, HEAD_DIM),
            k_a_s.reshape(db, CFG_A.hist, CFG_A.n_kv, HEAD_DIM),
            v_a_s.reshape(db, CFG_A.hist, CFG_A.n_kv, HEAD_DIM),
            k_b_s.reshape(db, CFG_B.hist, CFG_B.n_kv, HEAD_DIM),
            v_b_s.reshape(db, CFG_B.hist, CFG_B.n_kv, HEAD_DIM))
```
